```python
import jax, jax.numpy as jnp
from jax import lax
import numpy as np

D_MODEL = 2048
BATCH = 16
SEQ = 256
DEPTH = 2
DEC_BATCH = 4
DEC_SEQ = 1024
PAST_LEN = 256

GRID_W = 64
N_MIXERS = 2
N_A_LAYERS = (DEPTH + 1) // 2
N_B_LAYERS = DEPTH // 2
RWKV_HEAD = 64
RWKV_HEADS = D_MODEL // RWKV_HEAD
DECAY_LORA = 96
AAA_LORA = 96
GATE_LORA = 256
N_SHIFT = 6
POOL_WINDOWS = (2, 4, 8, 16)
N_POOL_GROUPS = len(POOL_WINDOWS)
POOL_GROUP = D_MODEL // N_POOL_GROUPS
N_EXPERTS = 64
TOP_K = 6
N_GROUPS = 8
TOPK_GROUPS = 4
EXPERT_DIM = 512
SHARED_DIM = 512
ROUTED_SCALE = 2.5
ALPHA = (2 * DEPTH) ** 0.25
BETA = (8 * DEPTH) ** -0.25
LN_EPS = 1e-5
GN_EPS = 64e-5

kernel_name = 'bidir_rwkv7_pool_moe_prefix_step'


def layer_norm(x, g, b):
    xf = x.astype(jnp.float32)
    mu = jnp.mean(xf, axis=-1, keepdims=True)
    var = jnp.mean(jnp.square(xf - mu), axis=-1, keepdims=True)
    return ((xf - mu) * lax.rsqrt(var + LN_EPS) * g + b).astype(x.dtype)


def ada_params(cond, w_ada_l, b_ada_l):
    m = jax.nn.silu(cond) @ w_ada_l + b_ada_l
    return jnp.split(m[:, None, :], 6, axis=-1)


def modulate(x, shift, scale):
    return x * (1.0 + scale) + shift


def wkv_scan(S0, r, decay, kk, kka, v, k, reverse):
    def step(S, inp):
        r_t, w_t, kk_t, kka_t, v_t, k_t = inp
        sa = jnp.einsum('bhij,bhj->bhi', S, kk_t)
        S = (S * w_t[:, :, None, :]
             - jnp.einsum('bhi,bhj->bhij', sa, kka_t)
             + jnp.einsum('bhi,bhj->bhij', v_t, k_t))
        return S, jnp.einsum('bhij,bhj->bhi', S, r_t)
    xs = tuple(jnp.swapaxes(t, 0, 1) for t in (r, decay, kk, kka, v, k))
    S, ys = lax.scan(step, S0, xs, reverse=reverse)
    return S, jnp.swapaxes(ys, 0, 1)


def rwkv_mixer(x, S0, mix_prev, mix_next, w_r, w_k, w_v, w_o, w0, w1, w2,
               a0, a1, a2, g1, g2, k_k, k_a, r_k, gn_g, gn_b):
    B, L, D = x.shape
    H, N = RWKV_HEADS, RWKV_HEAD
    f32 = jnp.float32
    d_prev = jnp.pad(x[:, :-1], ((0, 0), (1, 0), (0, 0))) - x
    d_next = jnp.pad(x[:, 1:], ((0, 0), (0, 1), (0, 0))) - x
    xs = x[:, :, None, :] + mix_prev * d_prev[:, :, None, :] + mix_next * d_next[:, :, None, :]
    xr, xw, xk, xv, xa, xg = (xs[:, :, i] for i in range(N_SHIFT))
    to_heads = lambda t: t.reshape(B, L, H, N)
    r = to_heads((xr @ w_r).astype(f32))
    k = (xk @ w_k).astype(f32)
    v = to_heads((xv @ w_v).astype(f32))
    g = jax.nn.sigmoid(xg @ g1) @ g2
    kk = to_heads(k * k_k)
    kk = kk / jnp.maximum(jnp.linalg.norm(kk, axis=-1, keepdims=True), 1e-12)
    if S0 is None:
        S0 = jnp.zeros((B, 2, H, N, N), f32)
    wkv_parts, bonus_parts, finals = [], [], []
    for d in range(2):
        w_log = -jax.nn.softplus(-(w0[d] + jnp.tanh(xw @ w1[d]) @ w2[d]).astype(f32)) - 0.5
        decay = to_heads(jnp.exp(-jnp.exp(w_log)))
        a = jax.nn.sigmoid((a0[d] + (xa @ a1[d]) @ a2[d]).astype(f32))
        k_d = to_heads(k * (1.0 + (a - 1.0) * k_a))
        a = to_heads(a)
        S_fin, y = wkv_scan(S0[:, d].astype(f32), r, decay, kk, kk * a, v, k_d, d == 1)
        wkv_parts.append(y)
        bonus_parts.append(jnp.sum(r * k_d * r_k, axis=-1, keepdims=True) * v)
        finals.append(S_fin)
    y = wkv_parts[0] + wkv_parts[1]
    mu = jnp.mean(y, axis=-1, keepdims=True)
    var = jnp.mean(jnp.square(y - mu), axis=-1, keepdims=True)
    yn = ((y - mu) * lax.rsqrt(var + GN_EPS)).reshape(B, L, D) * gn_g + gn_b
    out = (yn + (bonus_parts[0] + bonus_parts[1]).reshape(B, L, D)) * g
    return out.astype(x.dtype) @ w_o, jnp.stack(finals, axis=1)


def box_sum_1d(x, w, axis):
    L = x.shape[axis]
    cs = jnp.cumsum(x, axis=axis)
    pad = [(0, 0)] * x.ndim
    pad[axis] = (1, 0)
    cs = jnp.pad(cs, pad)
    t = np.arange(L)
    lo = np.clip(t - w // 2, 0, L)
    hi = np.clip(t - w // 2 + w, 0, L)
    s = jnp.take(cs, hi, axis=axis) - jnp.take(cs, lo, axis=axis)
    return s, (hi - lo).astype(np.float32)


def pool_mixer(x, w_pool, pool_scale, grid):
    B, L, D = x.shape
    xf = x.astype(jnp.float32)
    outs = []
    for gi, w in enumerate(POOL_WINDOWS):
        xg = xf[..., gi * POOL_GROUP:(gi + 1) * POOL_GROUP]
        if grid:
            rows = L // GRID_W
            s, cr = box_sum_1d(xg.reshape(B, rows, GRID_W, POOL_GROUP), w, 1)
            s, cc = box_sum_1d(s, w, 2)
            mean = (s / (cr[:, None, None] * cc[None, :, None])).reshape(B, L, POOL_GROUP)
        else:
            s, cnt = box_sum_1d(xg, w, 1)
            mean = s / cnt[:, None]
        outs.append(mean - xg)
    p = jnp.stack(outs, axis=2).astype(x.dtype)
    y = jnp.einsum('blgc,gcd->blgd', p, w_pool).reshape(B, L, D)
    return y * pool_scale


def moe(x, w_router, router_bias, w_gate, w_up, w_down, ws_gate, ws_up, ws_down):
    B, L, D = x.shape
    t = x.reshape(B * L, D)
    T = t.shape[0]
    scores = jax.nn.sigmoid((t @ w_router).astype(jnp.float32))
    biased = scores + router_bias
    grp = biased.reshape(T, N_GROUPS, N_EXPERTS // N_GROUPS)
    grp_score = jnp.sum(lax.top_k(grp, 2)[0], axis=-1)
    _, gidx = lax.top_k(grp_score, TOPK_GROUPS)
    gmask = jnp.sum(jax.nn.one_hot(gidx, N_GROUPS, dtype=jnp.float32), axis=-2)
    emask = jnp.repeat(gmask, N_EXPERTS // N_GROUPS, axis=-1)
    masked = jnp.where(emask > 0, biased, -jnp.inf)
    _, eidx = lax.top_k(masked, TOP_K)
    w_sel = jnp.take_along_axis(scores, eidx, axis=-1)
    w_sel = w_sel / (jnp.sum(w_sel, axis=-1, keepdims=True) + 1e-20) * ROUTED_SCALE
    gates = jnp.sum(jax.nn.one_hot(eidx, N_EXPERTS, dtype=jnp.float32) * w_sel[..., None], axis=-2)
    shared = (jax.nn.silu(t @ ws_gate) * (t @ ws_up)) @ ws_down

    def body(acc, ew):
        wg, wu, wd, ge = ew
        h = jax.nn.silu(t @ wg) * (t @ wu)
        return acc + (h * ge[:, None].astype(h.dtype)) @ wd, None

    out, _ = lax.scan(body, shared, (w_gate, w_up, w_down, gates.T))
    return out.reshape(B, L, D)


def trunk(x, cond, S0_all, grid, w_ada, b_ada, ln_g, ln_b, rwkv_params, pool_params, moe_params):
    states = []
    for l in range(DEPTH):
        sh1, sc1, gt1, sh2, sc2, gt2 = ada_params(cond, w_ada[l], b_ada[l])
        h = modulate(x, sh1, sc1)
        j = l // N_MIXERS
        if l % N_MIXERS == 0:
            S0 = None if S0_all is None else S0_all[:, j]
            mix, S_fin = rwkv_mixer(h, S0, *[p[j] for p in rwkv_params])
            states.append(S_fin)
        else:
            mix = pool_mixer(h, pool_params[0][j], pool_params[1][j], grid)
        x = layer_norm(ALPHA * x + gt1 * mix, ln_g[l, 0], ln_b[l, 0])
        h = modulate(x, sh2, sc2)
        x = layer_norm(ALPHA * x + gt2 * moe(h, *[p[l] for p in moe_params]), ln_g[l, 1], ln_b[l, 1])
    return x, states


def setup_inputs(seed: int = 0) -> dict:
    key = jax.random.key(seed)
    ks = iter(jax.random.split(key, 48))
    nrm = lambda shape, s: jax.random.normal(next(ks), shape, jnp.float32) * s
    D, H, N, E = D_MODEL, RWKV_HEADS, RWKV_HEAD, N_EXPERTS
    return {
        'x_prompt': nrm((BATCH, SEQ, D), 1.0),
        'x_sample': nrm((DEC_BATCH, DEC_SEQ, D), 1.0),
        'c': nrm((DEC_BATCH, D), 1.0),
        'state_rwkv': nrm((DEC_BATCH, N_A_LAYERS, 2, H, N, N), 0.3),
        'c_ctx': nrm((D,), 1.0),
        'w_ada': nrm((DEPTH, D, 6 * D), 0.5 * D ** -0.5),
        'b_ada': nrm((DEPTH, 6 * D), 0.1),
        'ln_g': 1.0 + nrm((DEPTH, 2, D), 0.1),
        'ln_b': nrm((DEPTH, 2, D), 0.01),
        'rw_mix_prev': jax.random.uniform(next(ks), (N_A_LAYERS, N_SHIFT, D), jnp.float32, 0.0, 0.5),
        'rw_mix_next': jax.random.uniform(next(ks), (N_A_LAYERS, N_SHIFT, D), jnp.float32, 0.0, 0.5),
        'rw_w_r': nrm((N_A_LAYERS, D, D), D ** -0.5),
        'rw_w_k': nrm((N_A_LAYERS, D, D), D ** -0.5),
        'rw_w_v': nrm((N_A_LAYERS, D, D), D ** -0.5),
        'rw_w_o': nrm((N_A_LAYERS, D, D), BETA * D ** -0.5),
        'rw_w0': jax.random.uniform(next(ks), (N_A_LAYERS, 2, D), jnp.float32, -6.0, -1.0),
        'rw_w1': nrm((N_A_LAYERS, 2, D, DECAY_LORA), D ** -0.5),
        'rw_w2': nrm((N_A_LAYERS, 2, DECAY_LORA, D), 0.5 * DECAY_LORA ** -0.5),
        'rw_a0': nrm((N_A_LAYERS, 2, D), 0.5),
        'rw_a1': nrm((N_A_LAYERS, 2, D, AAA_LORA), D ** -0.5),
        'rw_a2': nrm((N_A_LAYERS, 2, AAA_LORA, D), 0.5 * AAA_LORA ** -0.5),
        'rw_g1': nrm((N_A_LAYERS, D, GATE_LORA), D ** -0.5),
        'rw_g2': nrm((N_A_LAYERS, GATE_LORA, D), GATE_LORA ** -0.5),
        'rw_k_k': 0.85 + nrm((N_A_LAYERS, D), 0.05),
        'rw_k_a': 1.0 + nrm((N_A_LAYERS, D), 0.05),
        'rw_r_k': nrm((N_A_LAYERS, H, N), 0.1),
        'rw_gn_g': 1.0 + nrm((N_A_LAYERS, D), 0.1),
        'rw_gn_b': nrm((N_A_LAYERS, D), 0.01),
        'pool_w': nrm((N_B_LAYERS, N_POOL_GROUPS, POOL_GROUP, POOL_GROUP), BETA * POOL_GROUP ** -0.5),
        'pool_scale': 1.0 + nrm((N_B_LAYERS, D), 0.1),
        'moe_router': nrm((DEPTH, D, E), D ** -0.5),
        'moe_router_bias': nrm((DEPTH, E), 0.01),
        'moe_w_gate': nrm((DEPTH, E, D, EXPERT_DIM), D ** -0.5),
        'moe_w_up': nrm((DEPTH, E, D, EXPERT_DIM), D ** -0.5),
        'moe_w_down': nrm((DEPTH, E, EXPERT_DIM, D), BETA * EXPERT_DIM ** -0.5),
        'moe_ws_gate': nrm((DEPTH, D, SHARED_DIM), D ** -0.5),
        'moe_ws_up': nrm((DEPTH, D, SHARED_DIM), D ** -0.5),
        'moe_ws_down': nrm((DEPTH, SHARED_DIM, D), BETA * SHARED_DIM ** -0.5),
    }


def reference(x_prompt, x_sample, c, state_rwkv, c_ctx, w_ada, b_ada, ln_g, ln_b,
              rw_mix_prev, rw_mix_next, rw_w_r, rw_w_k, rw_w_v, rw_w_o, rw_w0, rw_w1, rw_w2,
              rw_a0, rw_a1, rw_a2, rw_g1, rw_g2, rw_k_k, rw_k_a, rw_r_k, rw_gn_g, rw_gn_b,
              pool_w, pool_scale, moe_router, moe_router_bias, moe_w_gate, moe_w_up, moe_w_down,
              moe_ws_gate, moe_ws_up, moe_ws_down):
    rwkv_params = (rw_mix_prev, rw_mix_next, rw_w_r, rw_w_k, rw_w_v, rw_w_o, rw_w0, rw_w1, rw_w2,
                   rw_a0, rw_a1, rw_a2, rw_g1, rw_g2, rw_k_k, rw_k_a, rw_r_k, rw_gn_g, rw_gn_b)
    pool_params = (pool_w, pool_scale)
    moe_params = (moe_router, moe_router_bias, moe_w_gate, moe_w_up, moe_w_down,
                  moe_ws_gate, moe_ws_up, moe_ws_down)
    y_prompt, ctx_states = trunk(x_prompt, c_ctx[None, :], None, False, w_ada, b_ada, ln_g, ln_b,
                                 rwkv_params, pool_params, moe_params)
    new_state_rwkv = jnp.stack(ctx_states, axis=1)
    y_sample, _ = trunk(x_sample, c, state_rwkv, True, w_ada, b_ada, ln_g, ln_b,
                        rwkv_params, pool_params, moe_params)
    return (y_prompt, y_sample, new_state_rwkv)
```

```python
import functools

import jax
import jax.numpy as jnp
from jax import lax
from jax.experimental import pallas as pl
from jax.experimental.pallas import tpu as pltpu

D_MODEL = 2048
DEPTH = 2
GRID_W = 64
RWKV_HEAD = 64
RWKV_HEADS = D_MODEL // RWKV_HEAD
POOL_WINDOWS = (2, 4, 8, 16)
POOL_GROUP = D_MODEL // len(POOL_WINDOWS)
N_EXPERTS = 64
TOP_K = 6
N_GROUPS = 8
TOPK_GROUPS = 4
ROUTED_SCALE = 2.5
ALPHA = (2 * DEPTH) ** 0.25
LN_EPS = 1e-5
GN_EPS = 64e-5

F32 = jnp.float32
BF16 = jnp.bfloat16

VMEM_LIMIT_BYTES = 48 * 1024 * 1024

SCAN_CHUNK = 64
SCAN_HEADS = 4
SCAN_LANES = SCAN_HEADS * RWKV_HEAD
MOE_TILE = 256


def _dot(a, b):
    return jnp.dot(a, b, preferred_element_type=F32)


def _dot_nt(a, b):
    return lax.dot_general(a, b, (((1,), (1,)), ((), ())), preferred_element_type=F32)


def _dot_tn(a, b):
    return lax.dot_general(a, b, (((0,), (0,)), ((), ())), preferred_element_type=F32)


def _mm_kernel(x_ref, w_ref, o_ref):
    o_ref[...] = _dot(x_ref[...].astype(BF16), w_ref[...].astype(BF16))


def matmul(x, w, *, tm=1024, tn=512):
    m, k = x.shape
    n = w.shape[1]
    tm = min(tm, m)
    tn = min(tn, n)
    assert m % tm == 0 and n % tn == 0
    return pl.pallas_call(
        _mm_kernel,
        out_shape=jax.ShapeDtypeStruct((m, n), F32),
        grid=(n // tn, m // tm),
        in_specs=[pl.BlockSpec((tm, k), lambda j, i: (i, 0)),
                  pl.BlockSpec((k, tn), lambda j, i: (0, j))],
        out_specs=pl.BlockSpec((tm, tn), lambda j, i: (i, j)),
        compiler_params=pltpu.CompilerParams(
            dimension_semantics=("arbitrary", "arbitrary"),
            vmem_limit_bytes=VMEM_LIMIT_BYTES),
        name="matmul",
    )(x, w)


def _split3(x):
    hi = x.astype(BF16)
    r1 = x - hi.astype(F32)
    mid = r1.astype(BF16)
    lo = (r1 - mid.astype(F32)).astype(BF16)
    return hi, mid, lo


def _scan_kernel(r_ref, kk_ref, v_ref, lw_ref, a_ref, kd_ref, s0_ref, y_ref, sfin_ref, s_scr, *, nc):
    C, N, HB, W = SCAN_CHUNK, RWKV_HEAD, SCAN_HEADS, SCAN_LANES
    t_idx = lax.broadcasted_iota(jnp.int32, (C, W), 0)
    s_idx = lax.broadcasted_iota(jnp.int32, (C, W), 1) % C
    eye_row = (s_idx == t_idx).astype(F32)
    tri_t = lax.broadcasted_iota(jnp.int32, (C, C), 0)
    tri_s = lax.broadcasted_iota(jnp.int32, (C, C), 1)
    blk_r = lax.broadcasted_iota(jnp.int32, (W, W), 0) // N
    blk_c = lax.broadcasted_iota(jnp.int32, (W, W), 1) // N
    bd_mask = blk_r == blk_c
    bd_mask_bf = bd_mask.astype(BF16)

    def expand(x_row_bf):
        return jnp.concatenate([x_row_bf] * HB, axis=0) * bd_mask_bf

    for d in range(2):
        s_scr[d] = jnp.zeros((W, W), F32)
        for h in range(HB):
            s_scr[d, h * N:(h + 1) * N, h * N:(h + 1) * N] = s0_ref[d, h]
    y_ref[...] = jnp.zeros_like(y_ref)

    def chunk_step(d, row0):
        fwd = d == 0
        strict = (s_idx < t_idx) if fwd else (s_idx > t_idx)
        incl = (s_idx <= t_idx) if fwd else (s_idx >= t_idx)
        tri = ((tri_s <= tri_t) if fwd else (tri_s >= tri_t)).astype(BF16)
        rows = pl.ds(row0, C)
        r = r_ref[rows, :]
        kk = kk_ref[rows, :]
        v = v_ref[rows, :]
        lw = lw_ref[d, rows, :]
        a = a_ref[d, rows, :]
        kd = kd_ref[d, rows, :]

        hi, mid, lo = _split3(lw)
        lc = _dot(tri, hi) + _dot(tri, mid) + _dot(tri, lo)
        g_in = jnp.exp(lc)
        g_inv = jnp.exp(-lc)
        g_ex = jnp.exp(lc - lw)
        last = C - 1 if fwd else 0
        g_end = g_in[last:last + 1, :]

        bt = (kk * g_ex).astype(BF16)
        rt = (r * g_in).astype(BF16)
        at_f = -(kk * a) * g_inv
        kt_f = kd * g_inv
        at = at_f.astype(BF16)
        kt = kt_f.astype(BF16)
        v_bf = v.astype(BF16)
        br = jnp.concatenate([bt, rt], axis=0)

        la = _dot_nt(br, expand(at))
        lk = _dot_nt(br, expand(kt))
        lba = jnp.where(strict, la[:C], 0.0)
        lra = jnp.where(incl, la[C:], 0.0)
        lbk = jnp.where(strict, lk[:C], 0.0)
        lrk = jnp.where(incl, lk[C:], 0.0)

        def off(b):
            tb, sb = t_idx // b, s_idx // b
            if fwd:
                return jnp.where((tb % 2 == 1) & (sb == tb - 1), lba, 0.0)
            return jnp.where((tb % 2 == 0) & (sb == tb + 1), lba, 0.0)

        inv = eye_row + off(1)
        b = 2
        while b < C:
            m1 = _dot(inv.astype(BF16), expand(off(b).astype(BF16)))
            inv = inv + _dot(m1.astype(BF16), expand(inv.astype(BF16)))
            b *= 2

        s_bd = s_scr[d]
        wy = _dot_nt(br, s_bd.astype(BF16))
        wy2 = _dot(jnp.concatenate([lbk.astype(BF16), lrk.astype(BF16)], axis=0), expand(v_bf))
        w_row = wy[:C] + wy2[:C]
        u_row = _dot(inv.astype(BF16), expand(w_row.astype(BF16)))
        u_bf = u_row.astype(BF16)
        y = wy[C:] + wy2[C:] + _dot(lra.astype(BF16), expand(u_bf))
        y_ref[rows, :] += y

        upd = _dot_tn(jnp.concatenate([u_bf, v_bf], axis=0), jnp.concatenate([at, kt], axis=0))
        s_scr[d] = jnp.where(bd_mask, (s_bd + upd) * g_end, 0.0)

    def body(c, carry):
        chunk_step(0, pl.multiple_of(c * C, C))
        chunk_step(1, pl.multiple_of((nc - 1 - c) * C, C))
        return carry

    lax.fori_loop(0, nc, body, 0)

    for d in range(2):
        for h in range(HB):
            sfin_ref[d, h] = s_scr[d, h * N:(h + 1) * N, h * N:(h + 1) * N]


def wkv_scan_pair(r, kk, v, lw, a, kd, s0):
    b, l, d = r.shape
    nc = l // SCAN_CHUNK
    g = d // SCAN_LANES
    seq = pl.BlockSpec((None, l, SCAN_LANES), lambda i, j: (i, 0, j))
    seq2 = pl.BlockSpec((2, None, l, SCAN_LANES), lambda i, j: (0, i, 0, j))
    st = pl.BlockSpec((None, 2, SCAN_HEADS, RWKV_HEAD, RWKV_HEAD), lambda i, j: (i, 0, j, 0, 0))
    return pl.pallas_call(
        functools.partial(_scan_kernel, nc=nc),
        out_shape=(jax.ShapeDtypeStruct((b, l, d), F32), jax.ShapeDtypeStruct(s0.shape, F32)),
        grid=(b, g),
        in_specs=[seq, seq, seq, seq2, seq2, seq2, st],
        out_specs=(seq, st),
        scratch_shapes=[pltpu.VMEM((2, SCAN_LANES, SCAN_LANES), F32)],
        compiler_params=pltpu.CompilerParams(
            dimension_semantics=("arbitrary", "arbitrary"),
            vmem_limit_bytes=VMEM_LIMIT_BYTES),
        name="wkv_scan",
    )(r, kk, v, lw, a, kd, s0)


def _ffn_kernel(tile_e_ref, nused_ref, x_ref, gate_ref, wg_ref, wu_ref, wd_ref, o_ref):
    del tile_e_ref
    i = pl.program_id(0)

    @pl.when(i < nused_ref[0])
    def _():
        x = x_ref[...]
        h1 = _dot(x, wg_ref[...].astype(BF16))
        h2 = _dot(x, wu_ref[...].astype(BF16))
        h = (h1 * jax.nn.sigmoid(h1)) * h2 * gate_ref[...]
        o_ref[...] = _dot(h.astype(BF16), wd_ref[...].astype(BF16))

    @pl.when(i >= nused_ref[0])
    def _():
        o_ref[...] = jnp.zeros_like(o_ref)


def grouped_ffn(x_rows, gate_rows, tile_e, nused, wg, wu, wd):
    rows, dm = x_rows.shape
    de = wg.shape[-1]
    n_tiles = rows // MOE_TILE

    def row_map(i, te, nu):
        return (jnp.minimum(i, nu[0] - 1), 0)

    def w_map(i, te, nu):
        return (te[i], 0, 0)

    return pl.pallas_call(
        _ffn_kernel,
        out_shape=jax.ShapeDtypeStruct((rows, dm), F32),
        grid_spec=pltpu.PrefetchScalarGridSpec(
            num_scalar_prefetch=2,
            grid=(n_tiles,),
            in_specs=[pl.BlockSpec((MOE_TILE, dm), row_map),
                      pl.BlockSpec((MOE_TILE, 1), row_map),
                      pl.BlockSpec((None, dm, de), w_map),
                      pl.BlockSpec((None, dm, de), w_map),
                      pl.BlockSpec((None, de, dm), w_map)],
            out_specs=pl.BlockSpec((MOE_TILE, dm), lambda i, te, nu: (i, 0))),
        compiler_params=pltpu.CompilerParams(
            dimension_semantics=("arbitrary",),
            vmem_limit_bytes=VMEM_LIMIT_BYTES),
        name="grouped_ffn",
    )(tile_e, nused, x_rows, gate_rows, wg, wu, wd)


def layer_norm(x, g, b):
    mu = jnp.mean(x, axis=-1, keepdims=True)
    var = jnp.mean(jnp.square(x - mu), axis=-1, keepdims=True)
    return (x - mu) * lax.rsqrt(var + LN_EPS) * g + b


def _route(t, w_router, router_bias):
    logits = jnp.dot(t, w_router, precision=lax.Precision.HIGHEST)
    scores = jax.nn.sigmoid(logits)
    biased = scores + router_bias
    tt = t.shape[0]
    grp = biased.reshape(tt, N_GROUPS, N_EXPERTS // N_GROUPS)
    grp_score = jnp.sum(lax.top_k(grp, 2)[0], axis=-1)
    _, gidx = lax.top_k(grp_score, TOPK_GROUPS)
    gmask = jnp.sum(jax.nn.one_hot(gidx, N_GROUPS, dtype=F32), axis=-2)
    emask = jnp.repeat(gmask, N_EXPERTS // N_GROUPS, axis=-1)
    masked = jnp.where(emask > 0, biased, -jnp.inf)
    _, eidx = lax.top_k(masked, TOP_K)
    w_sel = jnp.take_along_axis(scores, eidx, axis=-1)
    w_sel = w_sel / (jnp.sum(w_sel, axis=-1, keepdims=True) + 1e-20) * ROUTED_SCALE
    return eidx, w_sel


def moe(t, w_router, router_bias, w_gate, w_up, w_down, ws_gate, ws_up, ws_down):
    tt, dm = t.shape
    eidx, w_sel = _route(t, w_router, router_bias)
    t_bf = t.astype(BF16)

    n_pairs = tt * TOP_K
    max_rows = n_pairs + N_EXPERTS * MOE_TILE
    max_rows = (max_rows // MOE_TILE) * MOE_TILE
    flat_e = eidx.reshape(-1)
    counts = jnp.sum(jax.nn.one_hot(flat_e, N_EXPERTS, dtype=jnp.int32), axis=0)
    padded = ((counts + MOE_TILE - 1) // MOE_TILE) * MOE_TILE
    start_pad = jnp.cumsum(padded) - padded
    start_raw = jnp.cumsum(counts) - counts
    order = jnp.argsort(flat_e, stable=True)
    sorted_e = flat_e[order]
    slot = start_pad[sorted_e] + (jnp.arange(n_pairs, dtype=jnp.int32) - start_raw[sorted_e])
    tok_slot = jnp.zeros((max_rows,), jnp.int32).at[slot].set(order // TOP_K)
    gate_slot = jnp.zeros((max_rows,), F32).at[slot].set(w_sel.reshape(-1)[order])
    slot_of_pair = jnp.zeros((n_pairs,), jnp.int32).at[order].set(slot)
    end_pad = jnp.cumsum(padded)
    n_tiles = max_rows // MOE_TILE
    tile_start = jnp.arange(n_tiles, dtype=jnp.int32) * MOE_TILE
    tile_e = jnp.minimum(jnp.searchsorted(end_pad, tile_start, side="right"), N_EXPERTS - 1).astype(jnp.int32)
    nused = (end_pad[-1] // MOE_TILE).astype(jnp.int32).reshape(1)

    x_rows = t_bf[tok_slot]
    y_rows = grouped_ffn(x_rows, gate_slot[:, None], tile_e, nused, w_gate, w_up, w_down)
    routed = jnp.sum(y_rows[slot_of_pair.reshape(tt, TOP_K)], axis=1)

    s_tiles = tt // MOE_TILE
    shared = grouped_ffn(t_bf, jnp.ones((tt, 1), F32), jnp.zeros((s_tiles,), jnp.int32),
                         jnp.full((1,), s_tiles, jnp.int32), ws_gate[None], ws_up[None], ws_down[None])
    return shared + routed


def box_mean_1d(x, w, axis):
    n = x.shape[axis]
    lo_off = -(w // 2)
    acc = jnp.zeros_like(x)
    idx = jnp.arange(n)
    shape = [1] * x.ndim
    shape[axis] = n
    for off in range(lo_off, lo_off + w):
        shifted = jnp.roll(x, -off, axis=axis)
        valid = ((idx + off >= 0) & (idx + off < n)).reshape(shape)
        acc = acc + jnp.where(valid, shifted, 0.0)
    cnt = (jnp.minimum(idx + lo_off + w, n) - jnp.maximum(idx + lo_off, 0)).astype(F32).reshape(shape)
    return acc, cnt


def pool_pre(x, grid):
    b, l, dm = x.shape
    outs = []
    for gi, w in enumerate(POOL_WINDOWS):
        xg = x[..., gi * POOL_GROUP:(gi + 1) * POOL_GROUP]
        if grid:
            rows = l // GRID_W
            s, cr = box_mean_1d(xg.reshape(b, rows, GRID_W, POOL_GROUP), w, 1)
            s, cc = box_mean_1d(s, w, 2)
            mean = (s / (cr * cc)).reshape(b, l, POOL_GROUP)
        else:
            s, cnt = box_mean_1d(xg, w, 1)
            mean = s / cnt
        outs.append(mean - xg)
    return jnp.concatenate(outs, axis=-1)


def kernel(x_prompt, x_sample, c, state_rwkv, c_ctx, w_ada, b_ada, ln_g, ln_b, rw_mix_prev, rw_mix_next, rw_w_r, rw_w_k, rw_w_v, rw_w_o, rw_w0, rw_w1, rw_w2, rw_a0, rw_a1, rw_a2, rw_g1, rw_g2, rw_k_k, rw_k_a, rw_r_k, rw_gn_g, rw_gn_b, pool_w, pool_scale, moe_router, moe_router_bias, moe_w_gate, moe_w_up, moe_w_down, moe_ws_gate, moe_ws_up, moe_ws_down):
    dm = D_MODEL
    H, N = RWKV_HEADS, RWKV_HEAD
    bc, lc_, _ = x_prompt.shape
    bl, ll, _ = x_sample.shape
    tc = bc * lc_
    xs = [x_prompt, x_sample]
    cond = jnp.concatenate([c_ctx[None, :], c], axis=0)
    n_cond = cond.shape[0]
    cond_pad = jnp.zeros((8, dm), F32).at[:n_cond].set(jax.nn.silu(cond))
    new_states = []

    def flat(parts):
        return jnp.concatenate([p.reshape(-1, p.shape[-1]) for p in parts], axis=0)

    def unflat(t):
        return [t[:tc].reshape(bc, lc_, -1), t[tc:].reshape(bl, ll, -1)]

    for l in range(DEPTH):
        m = matmul(cond_pad, w_ada[l], tn=1024)[:n_cond] + b_ada[l]
        mods = jnp.split(m, 6, axis=-1)
        per = [[p[0:1][:, None, :] for p in mods], [p[1:][:, None, :] for p in mods]]
        hs = [xs[i] * (1.0 + per[i][1]) + per[i][0] for i in range(2)]
        j = l // 2
        if l % 2 == 0:
            mixes = []
            for h in hs:
                d_prev = jnp.pad(h[:, :-1], ((0, 0), (1, 0), (0, 0))) - h
                d_next = jnp.pad(h[:, 1:], ((0, 0), (0, 1), (0, 0))) - h
                mixes.append([(h + rw_mix_prev[j, i] * d_prev + rw_mix_next[j, i] * d_next).astype(BF16)
                              for i in range(6)])
            xr, xw, xk, xv, xa, xg = (flat([mixes[0][i], mixes[1][i]]) for i in range(6))
            r = matmul(xr, rw_w_r[j])
            k = matmul(xk, rw_w_k[j])
            v = matmul(xv, rw_w_v[j])
            g = matmul(jax.nn.sigmoid(matmul(xg, rw_g1[j])), rw_g2[j])
            kk = (k * rw_k_k[j]).reshape(-1, H, N)
            kk = (kk / jnp.maximum(jnp.linalg.norm(kk, axis=-1, keepdims=True), 1e-12)).reshape(-1, dm)
            lws, as_, kds, bonus = [], [], [], 0.0
            for d in range(2):
                w_log = -jax.nn.softplus(-(rw_w0[j, d] + matmul(jnp.tanh(matmul(xw, rw_w1[j, d])), rw_w2[j, d]))) - 0.5
                lws.append(-jnp.exp(w_log))
                a = jax.nn.sigmoid(rw_a0[j, d] + matmul(matmul(xa, rw_a1[j, d]), rw_a2[j, d]))
                k_d = k * (1.0 + (a - 1.0) * rw_k_a[j])
                as_.append(a)
                kds.append(k_d)
                rk = jnp.sum((r * k_d).reshape(-1, H, N) * rw_r_k[j], axis=-1, keepdims=True)
                bonus = bonus + (rk * v.reshape(-1, H, N)).reshape(-1, dm)
            lw_all, a_all, kd_all = jnp.stack(lws), jnp.stack(as_), jnp.stack(kds)
            ys = []
            s0s = [jnp.zeros((bc, 2, H, N, N), F32), state_rwkv[:, j]]
            for i, (sl, bb, ln_) in enumerate([(slice(0, tc), bc, lc_), (slice(tc, None), bl, ll)]):
                y, s_fin = wkv_scan_pair(
                    r[sl].reshape(bb, ln_, dm), kk[sl].reshape(bb, ln_, dm), v[sl].reshape(bb, ln_, dm),
                    lw_all[:, sl].reshape(2, bb, ln_, dm), a_all[:, sl].reshape(2, bb, ln_, dm),
                    kd_all[:, sl].reshape(2, bb, ln_, dm), s0s[i])
                ys.append(y.reshape(-1, dm))
                if i == 0:
                    new_states.append(s_fin)
            y = jnp.concatenate(ys, axis=0).reshape(-1, H, N)
            mu = jnp.mean(y, axis=-1, keepdims=True)
            var = jnp.mean(jnp.square(y - mu), axis=-1, keepdims=True)
            yn = ((y - mu) * lax.rsqrt(var + GN_EPS)).reshape(-1, dm) * rw_gn_g[j] + rw_gn_b[j]
            out = (yn + bonus) * g
            mix = unflat(matmul(out, rw_w_o[j]))
        else:
            p = flat([pool_pre(hs[0], False), pool_pre(hs[1], True)]).astype(BF16)
            cols = [matmul(p[:, gi * POOL_GROUP:(gi + 1) * POOL_GROUP], pool_w[j, gi]) for gi in range(len(POOL_WINDOWS))]
            mix = unflat(jnp.concatenate(cols, axis=-1) * pool_scale[j])
        xs = [layer_norm(ALPHA * xs[i] + per[i][2] * mix[i], ln_g[l, 0], ln_b[l, 0]) for i in range(2)]
        h2 = flat([xs[i] * (1.0 + per[i][4]) + per[i][3] for i in range(2)])
        mo = unflat(moe(h2, moe_router[l], moe_router_bias[l], moe_w_gate[l], moe_w_up[l], moe_w_down[l],
                        moe_ws_gate[l], moe_ws_up[l], moe_ws_down[l]))
        xs = [layer_norm(ALPHA * xs[i] + per[i][5] * mo[i], ln_g[l, 1], ln_b[l, 1]) for i in range(2)]

    return (xs[0], xs[1], jnp.stack(new_states, axis=1))
```

```python
import functools

import jax
import jax.numpy as jnp
from jax import lax
from jax.experimental import pallas as pl
from jax.experimental.pallas import tpu as pltpu

D_MODEL = 2048
DEPTH = 2
GRID_W = 64
RWKV_HEAD = 64
RWKV_HEADS = D_MODEL // RWKV_HEAD
POOL_WINDOWS = (2, 4, 8, 16)
POOL_GROUP = D_MODEL // len(POOL_WINDOWS)
N_EXPERTS = 64
TOP_K = 6
N_GROUPS = 8
TOPK_GROUPS = 4
ROUTED_SCALE = 2.5
ALPHA = (2 * DEPTH) ** 0.25
LN_EPS = 1e-5
GN_EPS = 64e-5

F32 = jnp.float32
BF16 = jnp.bfloat16

VMEM_LIMIT_BYTES = 48 * 1024 * 1024

SCAN_CHUNK = 64
SCAN_HEADS = 4
SCAN_LANES = SCAN_HEADS * RWKV_HEAD
MOE_TILE = 256


def _dot(a, b):
    return jnp.dot(a, b, preferred_element_type=F32)


def _dot_nt(a, b):
    return lax.dot_general(a, b, (((1,), (1,)), ((), ())), preferred_element_type=F32)


def _dot_tn(a, b):
    return lax.dot_general(a, b, (((0,), (0,)), ((), ())), preferred_element_type=F32)


def _mm_kernel(x_ref, w_ref, o_ref):
    o_ref[...] = _dot(x_ref[...].astype(BF16), w_ref[...].astype(BF16))


def matmul(x, w, *, tm=1024, tn=512):
    m, k = x.shape
    n = w.shape[1]
    tm = min(tm, m)
    tn = min(tn, n)
    assert m % tm == 0 and n % tn == 0
    return pl.pallas_call(
        _mm_kernel,
        out_shape=jax.ShapeDtypeStruct((m, n), F32),
        grid=(n // tn, m // tm),
        in_specs=[pl.BlockSpec((tm, k), lambda j, i: (i, 0)),
                  pl.BlockSpec((k, tn), lambda j, i: (0, j))],
        out_specs=pl.BlockSpec((tm, tn), lambda j, i: (i, j)),
        compiler_params=pltpu.CompilerParams(
            dimension_semantics=("arbitrary", "arbitrary"),
            vmem_limit_bytes=VMEM_LIMIT_BYTES),
        name="matmul",
    )(x, w)


def _split3(x):
    hi = x.astype(BF16)
    r1 = x - hi.astype(F32)
    mid = r1.astype(BF16)
    lo = (r1 - mid.astype(F32)).astype(BF16)
    return hi, mid, lo


def _scan_kernel(r_ref, kk_ref, v_ref, lw_ref, a_ref, kd_ref, s0_ref, y_ref, sfin_ref, s_scr, *, nc):
    C, N, HB, W = SCAN_CHUNK, RWKV_HEAD, SCAN_HEADS, SCAN_LANES
    t_idx = lax.broadcasted_iota(jnp.int32, (C, W), 0)
    s_idx = lax.broadcasted_iota(jnp.int32, (C, W), 1) % C
    eye_row = (s_idx == t_idx).astype(F32)
    tri_t = lax.broadcasted_iota(jnp.int32, (C, C), 0)
    tri_s = lax.broadcasted_iota(jnp.int32, (C, C), 1)
    blk_r = lax.broadcasted_iota(jnp.int32, (W, W), 0) // N
    blk_c = lax.broadcasted_iota(jnp.int32, (W, W), 1) // N
    bd_mask = blk_r == blk_c
    bd_mask_bf = bd_mask.astype(BF16)

    def expand(x_row_bf):
        return jnp.concatenate([x_row_bf] * HB, axis=0) * bd_mask_bf

    for d in range(2):
        s_scr[d] = jnp.zeros((W, W), F32)
        for h in range(HB):
            s_scr[d, h * N:(h + 1) * N, h * N:(h + 1) * N] = s0_ref[d, h]
    y_ref[...] = jnp.zeros_like(y_ref)

    def chunk_step(d, row0):
        fwd = d == 0
        strict = (s_idx < t_idx) if fwd else (s_idx > t_idx)
        incl = (s_idx <= t_idx) if fwd else (s_idx >= t_idx)
        tri = ((tri_s <= tri_t) if fwd else (tri_s >= tri_t)).astype(BF16)
        rows = pl.ds(row0, C)
        r = r_ref[rows, :]
        kk = kk_ref[rows, :]
        v = v_ref[rows, :]
        lw = lw_ref[d, rows, :]
        a = a_ref[d, rows, :]
        kd = kd_ref[d, rows, :]

        hi, mid, lo = _split3(lw)
        lc = _dot(tri, hi) + _dot(tri, mid) + _dot(tri, lo)
        g_in = jnp.exp(lc)
        g_inv = jnp.exp(-lc)
        g_ex = jnp.exp(lc - lw)
        last = C - 1 if fwd else 0
        g_end = g_in[last:last + 1, :]

        bt = (kk * g_ex).astype(BF16)
        rt = (r * g_in).astype(BF16)
        at_f = -(kk * a) * g_inv
        kt_f = kd * g_inv
        at = at_f.astype(BF16)
        kt = kt_f.astype(BF16)
        v_bf = v.astype(BF16)
        br = jnp.concatenate([bt, rt], axis=0)

        la = _dot_nt(br, expand(at))
        lk = _dot_nt(br, expand(kt))
        lba = jnp.where(strict, la[:C], 0.0)
        lra = jnp.where(incl, la[C:], 0.0)
        lbk = jnp.where(strict, lk[:C], 0.0)
        lrk = jnp.where(incl, lk[C:], 0.0)

        def off(b):
            tb, sb = t_idx // b, s_idx // b
            if fwd:
                return jnp.where((tb % 2 == 1) & (sb == tb - 1), lba, 0.0)
            return jnp.where((tb % 2 == 0) & (sb == tb + 1), lba, 0.0)

        inv = eye_row + off(1)
        b = 2
        while b < C:
            m1 = _dot(inv.astype(BF16), expand(off(b).astype(BF16)))
            inv = inv + _dot(m1.astype(BF16), expand(inv.astype(BF16)))
            b *= 2

        s_bd = s_scr[d]
        wy = _dot_nt(br, s_bd.astype(BF16))
        wy2 = _dot(jnp.concatenate([lbk.astype(BF16), lrk.astype(BF16)], axis=0), expand(v_bf))
        w_row = wy[:C] + wy2[:C]
        u_row = _dot(inv.astype(BF16), expand(w_row.astype(BF16)))
        u_bf = u_row.astype(BF16)
        y = wy[C:] + wy2[C:] + _dot(lra.astype(BF16), expand(u_bf))
        y_ref[rows, :] += y

        upd = _dot_tn(jnp.concatenate([u_bf, v_bf], axis=0), jnp.concatenate([at, kt], axis=0))
        s_scr[d] = jnp.where(bd_mask, (s_bd + upd) * g_end, 0.0)

    def body(c, carry):
        chunk_step(0, pl.multiple_of(c * C, C))
        chunk_step(1, pl.multiple_of((nc - 1 - c) * C, C))
        return carry

    lax.fori_loop(0, nc, body, 0)

    for d in range(2):
        for h in range(HB):
            sfin_ref[d, h] = s_scr[d, h * N:(h + 1) * N, h * N:(h + 1) * N]


def wkv_scan_pair(r, kk, v, lw, a, kd, s0):
    b, l, d = r.shape
    nc = l // SCAN_CHUNK
    g = d // SCAN_LANES
    seq = pl.BlockSpec((None, l, SCAN_LANES), lambda i, j: (i, 0, j))
    seq2 = pl.BlockSpec((2, None, l, SCAN_LANES), lambda i, j: (0, i, 0, j))
    st = pl.BlockSpec((None, 2, SCAN_HEADS, RWKV_HEAD, RWKV_HEAD), lambda i, j: (i, 0, j, 0, 0))
    return pl.pallas_call(
        functools.partial(_scan_kernel, nc=nc),
        out_shape=(jax.ShapeDtypeStruct((b, l, d), F32), jax.ShapeDtypeStruct(s0.shape, F32)),
        grid=(b, g),
        in_specs=[seq, seq, seq, seq2, seq2, seq2, st],
        out_specs=(seq, st),
        scratch_shapes=[pltpu.VMEM((2, SCAN_LANES, SCAN_LANES), F32)],
        compiler_params=pltpu.CompilerParams(
            dimension_semantics=("arbitrary", "arbitrary"),
            vmem_limit_bytes=VMEM_LIMIT_BYTES),
        name="wkv_scan",
    )(r, kk, v, lw, a, kd, s0)


def _split2(x):
    hi = x.astype(BF16)
    return hi, (x - hi.astype(F32)).astype(BF16)


def _first_index(hit_value, iota, sentinel, axis):
    return jnp.min(jnp.where(hit_value, iota, sentinel), axis=axis, keepdims=True)


def _router_kernel(h_ref, wrt_ref, bias_ref, eidx_ref, pos_ref, gate_ref, cnt_ref, cnt_scr):
    tm = h_ref.shape[0]
    E, G, GS = N_EXPERTS, N_GROUPS, N_EXPERTS // N_GROUPS
    neg = -jnp.inf

    @pl.when(pl.program_id(0) == 0)
    def _():
        cnt_scr[...] = jnp.zeros_like(cnt_scr)

    x_hi, x_lo = _split2(h_ref[...])
    w_hi, w_lo = _split2(wrt_ref[...])
    logits = _dot_nt(w_hi, x_hi) + (_dot_nt(w_hi, x_lo) + _dot_nt(w_lo, x_hi))
    scores = jax.nn.sigmoid(logits)
    biased = scores + bias_ref[...]

    g3 = biased.reshape(G, GS, tm)
    w_iota = lax.broadcasted_iota(jnp.int32, (G, GS, tm), 1).astype(F32)
    m1 = jnp.max(g3, axis=1, keepdims=True)
    first = _first_index(g3 == m1, w_iota, float(GS), 1)
    m2 = jnp.max(jnp.where(w_iota == first, neg, g3), axis=1, keepdims=True)
    gs = (m1 + m2).reshape(G, tm)

    g_iota = lax.broadcasted_iota(jnp.int32, (G, tm), 0).astype(F32)
    gsel = jnp.zeros((G, tm), jnp.bool_)
    cur = gs
    for _ in range(TOPK_GROUPS):
        m = jnp.max(cur, axis=0, keepdims=True)
        hit = g_iota == _first_index(cur == m, g_iota, float(G), 0)
        gsel = gsel | hit
        cur = jnp.where(hit, neg, cur)

    masked = jnp.where(gsel[:, None, :], g3, neg).reshape(E, tm)
    e_iota = lax.broadcasted_iota(jnp.int32, (E, tm), 0).astype(F32)
    sel = jnp.zeros((E, tm), jnp.bool_)
    hits, ids = [], []
    cur = masked
    for _ in range(TOP_K):
        m = jnp.max(cur, axis=0, keepdims=True)
        f = _first_index(cur == m, e_iota, float(E), 0)
        hit = e_iota == f
        hits.append(hit)
        ids.append(f)
        sel = sel | hit
        cur = jnp.where(hit, neg, cur)

    ssum = jnp.sum(jnp.where(sel, scores, 0.0), axis=0, keepdims=True)
    gates = scores / (ssum + 1e-20) * ROUTED_SCALE

    before = (lax.broadcasted_iota(jnp.int32, (tm, tm), 0) < lax.broadcasted_iota(jnp.int32, (tm, tm), 1))
    sel_f = sel.astype(F32)
    rank = _dot(sel_f.astype(BF16), before.astype(BF16)) + cnt_scr[:, 0:1]
    cnt_new = cnt_scr[...] + jnp.sum(sel_f, axis=1, keepdims=True)
    cnt_scr[...] = cnt_new
    cnt_ref[...] = cnt_new.astype(jnp.int32)

    k_iota = lax.broadcasted_iota(jnp.int32, (8, tm), 0)
    eidx8 = jnp.zeros((8, tm), F32)
    pos8 = jnp.zeros((8, tm), F32)
    gate8 = jnp.zeros((8, tm), F32)
    for k in range(TOP_K):
        row = k_iota == k
        eidx8 = jnp.where(row, ids[k], eidx8)
        pos8 = jnp.where(row, jnp.sum(jnp.where(hits[k], rank, 0.0), axis=0, keepdims=True), pos8)
        gate8 = jnp.where(row, jnp.sum(jnp.where(hits[k], gates, 0.0), axis=0, keepdims=True), gate8)
    eidx_ref[...] = eidx8.astype(jnp.int32)
    pos_ref[...] = pos8.astype(jnp.int32)
    gate_ref[...] = gate8


def route(h, w_router, router_bias):
    tt, dm = h.shape
    nt = tt // MOE_TILE
    tok = pl.BlockSpec((None, 8, MOE_TILE), lambda i: (i, 0, 0))
    eidx, pos, gate, cnt = pl.pallas_call(
        _router_kernel,
        out_shape=(jax.ShapeDtypeStruct((nt, 8, MOE_TILE), jnp.int32),
                   jax.ShapeDtypeStruct((nt, 8, MOE_TILE), jnp.int32),
                   jax.ShapeDtypeStruct((nt, 8, MOE_TILE), F32),
                   jax.ShapeDtypeStruct((N_EXPERTS, 128), jnp.int32)),
        grid=(nt,),
        in_specs=[pl.BlockSpec((MOE_TILE, dm), lambda i: (i, 0)),
                  pl.BlockSpec((N_EXPERTS, dm), lambda i: (0, 0)),
                  pl.BlockSpec((N_EXPERTS, 1), lambda i: (0, 0))],
        out_specs=(tok, tok, tok, pl.BlockSpec((N_EXPERTS, 128), lambda i: (0, 0))),
        scratch_shapes=[pltpu.VMEM((N_EXPERTS, 128), F32)],
        compiler_params=pltpu.CompilerParams(
            dimension_semantics=("arbitrary",), vmem_limit_bytes=VMEM_LIMIT_BYTES),
        name="moe_router",
    )(h, w_router.T, router_bias[:, None])
    return eidx, pos, gate, cnt[:, 0]


def _dispatch_kernel(start_ref, eidx_ref, pos_ref, x_ref, xs_ref, sem):
    tm = x_ref.shape[0]

    def issue(t, carry):
        for k in range(TOP_K):
            slot = start_ref[eidx_ref[k, t]] + pos_ref[k, t]
            pltpu.make_async_copy(x_ref.at[pl.ds(t, 1), :], xs_ref.at[pl.ds(slot, 1), :], sem).start()
        return carry

    lax.fori_loop(0, tm, issue, 0)
    for k in range(TOP_K):
        pltpu.make_async_copy(x_ref, xs_ref.at[pl.ds(0, tm), :], sem).wait()


def dispatch(h, start, eidx, pos):
    tt, dm = h.shape
    nt = tt // MOE_TILE
    smem_tok = pl.BlockSpec((None, 8, MOE_TILE), lambda i, st: (i, 0, 0), memory_space=pltpu.SMEM)
    return pl.pallas_call(
        _dispatch_kernel,
        out_shape=jax.ShapeDtypeStruct((tt * TOP_K, dm), F32),
        grid_spec=pltpu.PrefetchScalarGridSpec(
            num_scalar_prefetch=1,
            grid=(nt,),
            in_specs=[smem_tok, smem_tok, pl.BlockSpec((MOE_TILE, dm), lambda i, st: (i, 0))],
            out_specs=pl.BlockSpec(memory_space=pl.ANY),
            scratch_shapes=[pltpu.SemaphoreType.DMA]),
        compiler_params=pltpu.CompilerParams(
            dimension_semantics=("arbitrary",), vmem_limit_bytes=VMEM_LIMIT_BYTES),
        name="moe_dispatch",
    )(start, eidx, pos, h)


def _combine_kernel(start_ref, eidx_ref, pos_ref, gate_ref, base_ref, ys_ref, o_ref, buf, sem):
    tm = base_ref.shape[0]

    def issue(t, carry):
        for k in range(TOP_K):
            slot = start_ref[eidx_ref[k, t]] + pos_ref[k, t]
            pltpu.make_async_copy(ys_ref.at[pl.ds(slot, 1), :], buf.at[k, pl.ds(t, 1), :], sem).start()
        return carry

    lax.fori_loop(0, tm, issue, 0)
    for k in range(TOP_K):
        pltpu.make_async_copy(ys_ref.at[pl.ds(0, tm), :], buf.at[k], sem).wait()
    acc = base_ref[...]
    for k in range(TOP_K):
        acc = acc + gate_ref[:, k:k + 1] * buf[k]
    o_ref[...] = acc


def combine(base, ys, gate_cols, start, eidx, pos):
    tt, dm = base.shape
    nt = tt // MOE_TILE
    smem_tok = pl.BlockSpec((None, 8, MOE_TILE), lambda i, st: (i, 0, 0), memory_space=pltpu.SMEM)
    row = pl.BlockSpec((MOE_TILE, dm), lambda i, st: (i, 0))
    return pl.pallas_call(
        _combine_kernel,
        out_shape=jax.ShapeDtypeStruct((tt, dm), F32),
        grid_spec=pltpu.PrefetchScalarGridSpec(
            num_scalar_prefetch=1,
            grid=(nt,),
            in_specs=[smem_tok, smem_tok, pl.BlockSpec((MOE_TILE, 8), lambda i, st: (i, 0)), row,
                      pl.BlockSpec(memory_space=pl.ANY)],
            out_specs=row,
            scratch_shapes=[pltpu.VMEM((TOP_K, MOE_TILE, dm), F32), pltpu.SemaphoreType.DMA]),
        compiler_params=pltpu.CompilerParams(
            dimension_semantics=("arbitrary",), vmem_limit_bytes=VMEM_LIMIT_BYTES),
        name="moe_combine",
    )(start, eidx, pos, gate_cols, base, ys)


def _ffn_body(x, wg_ref, wu_ref, wd_ref, row_lo, row_hi):
    x = x.astype(BF16)
    h1 = _dot(x, wg_ref[...].astype(BF16))
    h2 = _dot(x, wu_ref[...].astype(BF16))
    h = (h1 * jax.nn.sigmoid(h1)) * h2
    if row_lo is not None:
        row = lax.broadcasted_iota(jnp.int32, h.shape, 0)
        h = jnp.where((row >= row_lo) & (row < row_hi), h, 0.0)
    return _dot(h.astype(BF16), wd_ref[...].astype(BF16))


def _ffn_items_kernel(tile_ref, exp_ref, lo_ref, hi_ref, first_ref, n_ref, x_ref, wg_ref, wu_ref, wd_ref, o_ref):
    del tile_ref, exp_ref
    w = pl.program_id(0)

    @pl.when(w < n_ref[0])
    def _():
        y = _ffn_body(x_ref[...], wg_ref, wu_ref, wd_ref, lo_ref[w], hi_ref[w])

        @pl.when(first_ref[w] == 1)
        def _():
            o_ref[...] = y

        @pl.when(first_ref[w] == 0)
        def _():
            o_ref[...] += y


def expert_ffn(xs, items, wg, wu, wd):
    rows, dm = xs.shape
    de = wg.shape[-1]
    n_items = items[0].shape[0]

    def row_map(w, tile, exp, lo, hi, first, n):
        return (tile[w], 0)

    def w_map(w, tile, exp, lo, hi, first, n):
        return (exp[w], 0, 0)

    return pl.pallas_call(
        _ffn_items_kernel,
        out_shape=jax.ShapeDtypeStruct((rows, dm), F32),
        grid_spec=pltpu.PrefetchScalarGridSpec(
            num_scalar_prefetch=6,
            grid=(n_items,),
            in_specs=[pl.BlockSpec((MOE_TILE, dm), row_map),
                      pl.BlockSpec((None, dm, de), w_map),
                      pl.BlockSpec((None, dm, de), w_map),
                      pl.BlockSpec((None, de, dm), w_map)],
            out_specs=pl.BlockSpec((MOE_TILE, dm), row_map)),
        compiler_params=pltpu.CompilerParams(
            dimension_semantics=("arbitrary",), vmem_limit_bytes=VMEM_LIMIT_BYTES),
        name="expert_ffn",
    )(*items, xs, wg, wu, wd)


def _ffn_dense_kernel(x_ref, wg_ref, wu_ref, wd_ref, o_ref):
    o_ref[...] = _ffn_body(x_ref[...], wg_ref, wu_ref, wd_ref, None, None)


def shared_ffn(x, wg, wu, wd):
    tt, dm = x.shape
    de = wg.shape[-1]
    row = pl.BlockSpec((MOE_TILE, dm), lambda i: (i, 0))
    return pl.pallas_call(
        _ffn_dense_kernel,
        out_shape=jax.ShapeDtypeStruct((tt, dm), F32),
        grid=(tt // MOE_TILE,),
        in_specs=[row, pl.BlockSpec((dm, de), lambda i: (0, 0)), pl.BlockSpec((dm, de), lambda i: (0, 0)),
                  pl.BlockSpec((de, dm), lambda i: (0, 0))],
        out_specs=row,
        compiler_params=pltpu.CompilerParams(
            dimension_semantics=("arbitrary",), vmem_limit_bytes=VMEM_LIMIT_BYTES),
        name="shared_ffn",
    )(x, wg, wu, wd)


def layer_norm(x, g, b):
    mu = jnp.mean(x, axis=-1, keepdims=True)
    var = jnp.mean(jnp.square(x - mu), axis=-1, keepdims=True)
    return (x - mu) * lax.rsqrt(var + LN_EPS) * g + b


def _ffn_items(counts, n_rows):
    n_tiles = n_rows // MOE_TILE
    n_items = n_tiles + N_EXPERTS - 1
    end = jnp.cumsum(counts)
    start = end - counts
    first_tile = start // MOE_TILE
    tiles_of = jnp.where(counts > 0, (end - 1) // MOE_TILE - first_tile + 1, 0)
    item_end = jnp.cumsum(tiles_of)
    item_start = item_end - tiles_of
    total = item_end[-1]
    w = jnp.minimum(jnp.arange(n_items, dtype=jnp.int32), total - 1)
    exp = jnp.sum((item_end[None, :] <= w[:, None]).astype(jnp.int32), axis=1)
    onehot = (exp[:, None] == jnp.arange(N_EXPERTS, dtype=jnp.int32)[None, :]).astype(jnp.int32)
    pick = lambda v: jnp.sum(onehot * v[None, :], axis=1)
    tile = pick(first_tile) + (w - pick(item_start))
    lo = jnp.maximum(pick(start) - tile * MOE_TILE, 0)
    hi = jnp.minimum(pick(end) - tile * MOE_TILE, MOE_TILE)
    prev_tile = jnp.concatenate([jnp.full((1,), -1, jnp.int32), tile[:-1]])
    first = (tile != prev_tile).astype(jnp.int32)
    i32 = lambda v: v.astype(jnp.int32)
    return start.astype(jnp.int32), (i32(tile), i32(exp), i32(lo), i32(hi), first, i32(total).reshape(1))


def moe(t, w_router, router_bias, w_gate, w_up, w_down, ws_gate, ws_up, ws_down):
    tt, dm = t.shape
    eidx, pos, gate, counts = route(t, w_router, router_bias)
    start, items = _ffn_items(counts, tt * TOP_K)
    xs = dispatch(t, start, eidx, pos)
    ys = expert_ffn(xs, items, w_gate, w_up, w_down)
    shared = shared_ffn(t, ws_gate, ws_up, ws_down)
    gate_cols = jnp.swapaxes(gate, 1, 2).reshape(tt, 8)
    return combine(shared, ys, gate_cols, start, eidx, pos)


def box_mean_1d(x, w, axis):
    n = x.shape[axis]
    lo_off = -(w // 2)
    acc = jnp.zeros_like(x)
    idx = jnp.arange(n)
    shape = [1] * x.ndim
    shape[axis] = n
    for off in range(lo_off, lo_off + w):
        shifted = jnp.roll(x, -off, axis=axis)
        valid = ((idx + off >= 0) & (idx + off < n)).reshape(shape)
        acc = acc + jnp.where(valid, shifted, 0.0)
    cnt = (jnp.minimum(idx + lo_off + w, n) - jnp.maximum(idx + lo_off, 0)).astype(F32).reshape(shape)
    return acc, cnt


def pool_pre(x, grid):
    b, l, dm = x.shape
    outs = []
    for gi, w in enumerate(POOL_WINDOWS):
        xg = x[..., gi * POOL_GROUP:(gi + 1) * POOL_GROUP]
        if grid:
            rows = l // GRID_W
            s, cr = box_mean_1d(xg.reshape(b, rows, GRID_W, POOL_GROUP), w, 1)
            s, cc = box_mean_1d(s, w, 2)
            mean = (s / (cr * cc)).reshape(b, l, POOL_GROUP)
        else:
            s, cnt = box_mean_1d(xg, w, 1)
            mean = s / cnt
        outs.append(mean - xg)
    return jnp.concatenate(outs, axis=-1)


def kernel(x_prompt, x_sample, c, state_rwkv, c_ctx, w_ada, b_ada, ln_g, ln_b, rw_mix_prev, rw_mix_next, rw_w_r, rw_w_k, rw_w_v, rw_w_o, rw_w0, rw_w1, rw_w2, rw_a0, rw_a1, rw_a2, rw_g1, rw_g2, rw_k_k, rw_k_a, rw_r_k, rw_gn_g, rw_gn_b, pool_w, pool_scale, moe_router, moe_router_bias, moe_w_gate, moe_w_up, moe_w_down, moe_ws_gate, moe_ws_up, moe_ws_down):
    dm = D_MODEL
    H, N = RWKV_HEADS, RWKV_HEAD
    bc, lc_, _ = x_prompt.shape
    bl, ll, _ = x_sample.shape
    tc = bc * lc_
    xs = [x_prompt, x_sample]
    cond = jnp.concatenate([c_ctx[None, :], c], axis=0)
    n_cond = cond.shape[0]
    cond_pad = jnp.zeros((8, dm), F32).at[:n_cond].set(jax.nn.silu(cond))
    new_states = []

    def flat(parts):
        return jnp.concatenate([p.reshape(-1, p.shape[-1]) for p in parts], axis=0)

    def unflat(t):
        return [t[:tc].reshape(bc, lc_, -1), t[tc:].reshape(bl, ll, -1)]

    for l in range(DEPTH):
        m = matmul(cond_pad, w_ada[l], tn=1024)[:n_cond] + b_ada[l]
        mods = jnp.split(m, 6, axis=-1)
        per = [[p[0:1][:, None, :] for p in mods], [p[1:][:, None, :] for p in mods]]
        hs = [xs[i] * (1.0 + per[i][1]) + per[i][0] for i in range(2)]
        j = l // 2
        if l % 2 == 0:
            mixes = []
            for h in hs:
                d_prev = jnp.pad(h[:, :-1], ((0, 0), (1, 0), (0, 0))) - h
                d_next = jnp.pad(h[:, 1:], ((0, 0), (0, 1), (0, 0))) - h
                mixes.append([(h + rw_mix_prev[j, i] * d_prev + rw_mix_next[j, i] * d_next).astype(BF16)
                              for i in range(6)])
            xr, xw, xk, xv, xa, xg = (flat([mixes[0][i], mixes[1][i]]) for i in range(6))
            r = matmul(xr, rw_w_r[j])
            k = matmul(xk, rw_w_k[j])
            v = matmul(xv, rw_w_v[j])
            g = matmul(jax.nn.sigmoid(matmul(xg, rw_g1[j])), rw_g2[j])
            kk = (k * rw_k_k[j]).reshape(-1, H, N)
            kk = (kk / jnp.maximum(jnp.linalg.norm(kk, axis=-1, keepdims=True), 1e-12)).reshape(-1, dm)
            lws, as_, kds, bonus = [], [], [], 0.0
            for d in range(2):
                w_log = -jax.nn.softplus(-(rw_w0[j, d] + matmul(jnp.tanh(matmul(xw, rw_w1[j, d])), rw_w2[j, d]))) - 0.5
                lws.append(-jnp.exp(w_log))
                a = jax.nn.sigmoid(rw_a0[j, d] + matmul(matmul(xa, rw_a1[j, d]), rw_a2[j, d]))
                k_d = k * (1.0 + (a - 1.0) * rw_k_a[j])
                as_.append(a)
                kds.append(k_d)
                rk = jnp.sum((r * k_d).reshape(-1, H, N) * rw_r_k[j], axis=-1, keepdims=True)
                bonus = bonus + (rk * v.reshape(-1, H, N)).reshape(-1, dm)
            lw_all, a_all, kd_all = jnp.stack(lws), jnp.stack(as_), jnp.stack(kds)
            ys = []
            s0s = [jnp.zeros((bc, 2, H, N, N), F32), state_rwkv[:, j]]
            for i, (sl, bb, ln_) in enumerate([(slice(0, tc), bc, lc_), (slice(tc, None), bl, ll)]):
                y, s_fin = wkv_scan_pair(
                    r[sl].reshape(bb, ln_, dm), kk[sl].reshape(bb, ln_, dm), v[sl].reshape(bb, ln_, dm),
                    lw_all[:, sl].reshape(2, bb, ln_, dm), a_all[:, sl].reshape(2, bb, ln_, dm),
                    kd_all[:, sl].reshape(2, bb, ln_, dm), s0s[i])
                ys.append(y.reshape(-1, dm))
                if i == 0:
                    new_states.append(s_fin)
            y = jnp.concatenate(ys, axis=0).reshape(-1, H, N)
            mu = jnp.mean(y, axis=-1, keepdims=True)
            var = jnp.mean(jnp.square(y - mu), axis=-1, keepdims=True)
            yn = ((y - mu) * lax.rsqrt(var + GN_EPS)).reshape(-1, dm) * rw_gn_g[j] + rw_gn_b[j]
            out = (yn + bonus) * g
            mix = unflat(matmul(out, rw_w_o[j]))
        else:
            p = flat([pool_pre(hs[0], False), pool_pre(hs[1], True)]).astype(BF16)
            cols = [matmul(p[:, gi * POOL_GROUP:(gi + 1) * POOL_GROUP], pool_w[j, gi]) for gi in range(len(POOL_WINDOWS))]
            mix = unflat(jnp.concatenate(cols, axis=-1) * pool_scale[j])
        xs = [layer_norm(ALPHA * xs[i] + per[i][2] * mix[i], ln_g[l, 0], ln_b[l, 0]) for i in range(2)]
        h2 = flat([xs[i] * (1.0 + per[i][4]) + per[i][3] for i in range(2)])
        mo = unflat(moe(h2, moe_router[l], moe_router_bias[l], moe_w_gate[l], moe_w_up[l], moe_w_down[l],
                        moe_ws_gate[l], moe_ws_up[l], moe_ws_down[l]))
        xs = [layer_norm(ALPHA * xs[i] + per[i][5] * mo[i], ln_g[l, 1], ln_b[l, 1]) for i in range(2)]

    return (xs[0], xs[1], jnp.stack(new_states, axis=1))
```

```python
import functools

import jax
import jax.numpy as jnp
from jax import lax
from jax.experimental import pallas as pl
from jax.experimental.pallas import tpu as pltpu

D_MODEL = 2048
DEPTH = 2
GRID_W = 64
RWKV_HEAD = 64
RWKV_HEADS = D_MODEL // RWKV_HEAD
POOL_WINDOWS = (2, 4, 8, 16)
POOL_GROUP = D_MODEL // len(POOL_WINDOWS)
N_EXPERTS = 64
TOP_K = 6
N_GROUPS = 8
TOPK_GROUPS = 4
ROUTED_SCALE = 2.5
ALPHA = (2 * DEPTH) ** 0.25
LN_EPS = 1e-5
GN_EPS = 64e-5

F32 = jnp.float32
BF16 = jnp.bfloat16

VMEM_LIMIT_BYTES = 48 * 1024 * 1024

SCAN_CHUNK = 64
SCAN_HEADS = 4
SCAN_LANES = SCAN_HEADS * RWKV_HEAD
SCAN_PREP_CHUNKS = 4
MOE_TILE = 256


def _dot(a, b):
    return jnp.dot(a, b, preferred_element_type=F32)


def _dot_nt(a, b):
    return lax.dot_general(a, b, (((1,), (1,)), ((), ())), preferred_element_type=F32)


def _dot_tn(a, b):
    return lax.dot_general(a, b, (((0,), (0,)), ((), ())), preferred_element_type=F32)


def _mm_kernel(x_ref, w_ref, o_ref):
    o_ref[...] = _dot(x_ref[...].astype(BF16), w_ref[...].astype(BF16))


def matmul(x, w, *, tm=1024, tn=512):
    m, k = x.shape
    n = w.shape[1]
    tm = min(tm, m)
    tn = min(tn, n)
    assert m % tm == 0 and n % tn == 0
    return pl.pallas_call(
        _mm_kernel,
        out_shape=jax.ShapeDtypeStruct((m, n), F32),
        grid=(n // tn, m // tm),
        in_specs=[pl.BlockSpec((tm, k), lambda j, i: (i, 0)),
                  pl.BlockSpec((k, tn), lambda j, i: (0, j))],
        out_specs=pl.BlockSpec((tm, tn), lambda j, i: (i, j)),
        compiler_params=pltpu.CompilerParams(
            dimension_semantics=("arbitrary", "arbitrary"),
            vmem_limit_bytes=VMEM_LIMIT_BYTES),
        name="matmul",
    )(x, w)


def _split3(x):
    hi = x.astype(BF16)
    r1 = x - hi.astype(F32)
    mid = r1.astype(BF16)
    lo = (r1 - mid.astype(F32)).astype(BF16)
    return hi, mid, lo


def _softplus(z):
    return jnp.maximum(z, 0.0) + jnp.log(1.0 + jnp.exp(-jnp.abs(z)))


def _scan_kernel(r_ref, k_ref, v_ref, g_ref, wl0_ref, wl1_ref, al0_ref, al1_ref, par_ref, s0_ref,
                 out_ref, sfin_ref, s_scr, kk_ref, lw_ref, a_ref, kd_ref, bonus_ref, y_ref,
                 br_s, ak_s, inv_s, lra_s, wy2_s, gend_s, *, nc, prep):
    C, N, HB, W = SCAN_CHUNK, RWKV_HEAD, SCAN_HEADS, SCAN_LANES
    t_idx = lax.broadcasted_iota(jnp.int32, (C, W), 0)
    s_idx = lax.broadcasted_iota(jnp.int32, (C, W), 1) % C
    eye_row = (s_idx == t_idx).astype(F32)
    tri_t = lax.broadcasted_iota(jnp.int32, (C, C), 0)
    tri_s = lax.broadcasted_iota(jnp.int32, (C, C), 1)
    blk_r = lax.broadcasted_iota(jnp.int32, (W, W), 0) // N
    blk_c = lax.broadcasted_iota(jnp.int32, (W, W), 1) // N
    bd_mask = blk_r == blk_c
    bd_mask_bf = bd_mask.astype(BF16)

    def expand(x_row_bf):
        return jnp.concatenate([x_row_bf] * HB, axis=0) * bd_mask_bf

    for d in range(2):
        s_scr[d] = jnp.zeros((W, W), F32)
        for h in range(HB):
            s_scr[d, h * N:(h + 1) * N, h * N:(h + 1) * N] = s0_ref[d, h]
    y_ref[...] = jnp.zeros_like(y_ref)

    ones_bd = bd_mask_bf
    RB = 4 * C
    par = lambda i: par_ref[i:i + 1, :]

    def head_sum(x):
        hi, lo = _split2(x)
        return _dot(hi, ones_bd) + _dot(lo, ones_bd)

    def prologue(i, carry):
        rows = pl.ds(pl.multiple_of(i * RB, RB), RB)
        k = k_ref[rows, :]
        kkp = k * par(4)
        kk_ref[rows, :] = kkp * lax.rsqrt(jnp.maximum(head_sum(kkp * kkp), 1e-24))
        kd_sum = jnp.zeros_like(k)
        for d, (wl_ref, al_ref) in enumerate(((wl0_ref, al0_ref), (wl1_ref, al1_ref))):
            w_log = -_softplus(-(par(d) + wl_ref[rows, :])) - 0.5
            lw_ref[d, rows, :] = -jnp.exp(w_log)
            a = jax.nn.sigmoid(par(2 + d) + al_ref[rows, :])
            a_ref[d, rows, :] = a
            kd = k * (1.0 + (a - 1.0) * par(5))
            kd_ref[d, rows, :] = kd
            kd_sum = kd_sum + kd
        bonus_ref[rows, :] = head_sum(r_ref[rows, :] * kd_sum * par(6)) * v_ref[rows, :]
        return carry

    lax.fori_loop(0, nc * C // RB, prologue, 0)

    def interleave(chains):
        live = list(chains)
        while live:
            nxt = []
            for ch in live:
                try:
                    next(ch)
                    nxt.append(ch)
                except StopIteration:
                    pass
            live = nxt

    def prepare_chunk(d, ci):
        fwd = d == 0
        strict = (s_idx < t_idx) if fwd else (s_idx > t_idx)
        incl = (s_idx <= t_idx) if fwd else (s_idx >= t_idx)
        tri = ((tri_s <= tri_t) if fwd else (tri_s >= tri_t)).astype(BF16)
        rows = pl.ds(pl.multiple_of(ci * C, C), C)
        lw = lw_ref[d, rows, :]
        hi, mid, lo = _split3(lw)
        lc = _dot(tri, hi) + _dot(tri, mid) + _dot(tri, lo)
        yield
        kk = kk_ref[rows, :]
        g_in = jnp.exp(lc)
        g_inv = jnp.exp(-lc)
        g_ex = jnp.exp(lc - lw)
        last = C - 1 if fwd else 0
        gend_s[d, ci] = jnp.broadcast_to(g_in[last:last + 1, :], (8, W))
        bt = (kk * g_ex).astype(BF16)
        rt = (r_ref[rows, :] * g_in).astype(BF16)
        at = (-(kk * a_ref[d, rows, :]) * g_inv).astype(BF16)
        kt = (kd_ref[d, rows, :] * g_inv).astype(BF16)
        br = jnp.concatenate([bt, rt], axis=0)
        br_s[d, ci] = br
        ak_s[d, ci] = jnp.concatenate([at, kt], axis=0)
        la = _dot_nt(br, expand(at))
        lk = _dot_nt(br, expand(kt))
        yield
        lba = jnp.where(strict, la[:C], 0.0)
        lra_s[d, ci] = jnp.where(incl, la[C:], 0.0).astype(BF16)
        lbrk = jnp.concatenate([jnp.where(strict, lk[:C], 0.0), jnp.where(incl, lk[C:], 0.0)], axis=0)
        wy2_s[d, ci] = _dot(lbrk.astype(BF16), expand(v_ref[rows, :].astype(BF16)))

        def off(b):
            tb, sb = t_idx // b, s_idx // b
            if fwd:
                return jnp.where((tb % 2 == 1) & (sb == tb - 1), lba, 0.0)
            return jnp.where((tb % 2 == 0) & (sb == tb + 1), lba, 0.0)

        inv = eye_row + off(1)
        b = 2
        while b < C:
            m1 = _dot(inv.astype(BF16), expand(off(b).astype(BF16)))
            yield
            inv = inv + _dot(m1.astype(BF16), expand(inv.astype(BF16)))
            yield
            b *= 2
        inv_s[d, ci] = inv.astype(BF16)

    def prepare(i, carry):
        interleave([prepare_chunk(d, i * prep + j) for j in range(prep) for d in range(2)])
        return carry

    lax.fori_loop(0, nc // prep, prepare, 0)

    def advance_chunk(d, ci):
        rows = pl.ds(pl.multiple_of(ci * C, C), C)
        s_bd = s_scr[d]
        wy = _dot_nt(br_s[d, ci], s_bd.astype(BF16))
        yield
        wy2 = wy2_s[d, ci]
        w_row = wy[:C] + wy2[:C]
        u_bf = _dot(inv_s[d, ci], expand(w_row.astype(BF16))).astype(BF16)
        yield
        y = wy[C:] + wy2[C:] + _dot(lra_s[d, ci], expand(u_bf))
        upd = _dot_tn(jnp.concatenate([u_bf, v_ref[rows, :].astype(BF16)], axis=0), ak_s[d, ci])
        yield
        y_ref[rows, :] += y
        s_scr[d] = jnp.where(bd_mask, (s_bd + upd) * gend_s[d, ci][0:1, :], 0.0)

    def advance(c, carry):
        interleave([advance_chunk(0, c), advance_chunk(1, nc - 1 - c)])
        return carry

    lax.fori_loop(0, nc, advance, 0)

    for d in range(2):
        for h in range(HB):
            sfin_ref[d, h] = s_scr[d, h * N:(h + 1) * N, h * N:(h + 1) * N]

    def epilogue(i, carry):
        rows = pl.ds(pl.multiple_of(i * RB, RB), RB)
        y = y_ref[rows, :]
        yc = y - head_sum(y) * (1.0 / N)
        var = head_sum(yc * yc) * (1.0 / N)
        yn = yc * lax.rsqrt(var + GN_EPS) * par(7) + par(8)
        out_ref[rows, :] = ((yn + bonus_ref[rows, :]) * g_ref[rows, :]).astype(BF16)
        return carry

    lax.fori_loop(0, nc * C // RB, epilogue, 0)


def rwkv_core(r, k, v, g, wl, al, par, s0):
    b, l, d = r.shape
    nc = l // SCAN_CHUNK
    seq = pl.BlockSpec((None, l, SCAN_LANES), lambda i, j: (i, 0, j))
    st = pl.BlockSpec((None, 2, SCAN_HEADS, RWKV_HEAD, RWKV_HEAD), lambda i, j: (i, 0, j, 0, 0))
    seq_scr = pltpu.VMEM((l, SCAN_LANES), F32)
    seq2_scr = pltpu.VMEM((2, l, SCAN_LANES), F32)
    C, W = SCAN_CHUNK, SCAN_LANES
    per_chunk = lambda rows, dt: pltpu.VMEM((2, nc, rows, W), dt)
    return pl.pallas_call(
        functools.partial(_scan_kernel, nc=nc, prep=SCAN_PREP_CHUNKS),
        out_shape=(jax.ShapeDtypeStruct((b, l, d), BF16), jax.ShapeDtypeStruct(s0.shape, F32)),
        grid=(b, d // SCAN_LANES),
        in_specs=[seq] * 8 + [pl.BlockSpec((16, SCAN_LANES), lambda i, j: (0, j)), st],
        out_specs=(seq, st),
        scratch_shapes=[pltpu.VMEM((2, SCAN_LANES, SCAN_LANES), F32),
                        seq_scr, seq2_scr, seq2_scr, seq2_scr, seq_scr, seq_scr,
                        per_chunk(2 * C, BF16), per_chunk(2 * C, BF16), per_chunk(C, BF16),
                        per_chunk(C, BF16), per_chunk(2 * C, F32), per_chunk(8, F32)],
        compiler_params=pltpu.CompilerParams(
            dimension_semantics=("arbitrary", "arbitrary"),
            vmem_limit_bytes=VMEM_LIMIT_BYTES),
        name="wkv_scan",
    )(r, k, v, g, wl[0], wl[1], al[0], al[1], par, s0)


def _split2(x):
    hi = x.astype(BF16)
    return hi, (x - hi.astype(F32)).astype(BF16)


def _first_index(hit_value, iota, sentinel, axis):
    return jnp.min(jnp.where(hit_value, iota, sentinel), axis=axis, keepdims=True)


def _router_kernel(h_ref, wrt_ref, bias_ref, eidx_ref, pos_ref, gate_ref, cnt_ref, cnt_scr):
    tm = h_ref.shape[0]
    E, G, GS = N_EXPERTS, N_GROUPS, N_EXPERTS // N_GROUPS
    neg = -jnp.inf

    @pl.when(pl.program_id(0) == 0)
    def _():
        cnt_scr[...] = jnp.zeros_like(cnt_scr)

    x_hi, x_lo = _split2(h_ref[...])
    w_hi, w_lo = _split2(wrt_ref[...])
    logits = _dot_nt(w_hi, x_hi) + (_dot_nt(w_hi, x_lo) + _dot_nt(w_lo, x_hi))
    scores = jax.nn.sigmoid(logits)
    biased = scores + bias_ref[...]

    g3 = biased.reshape(G, GS, tm)
    w_iota = lax.broadcasted_iota(jnp.int32, (G, GS, tm), 1).astype(F32)
    m1 = jnp.max(g3, axis=1, keepdims=True)
    first = _first_index(g3 == m1, w_iota, float(GS), 1)
    m2 = jnp.max(jnp.where(w_iota == first, neg, g3), axis=1, keepdims=True)
    gs = (m1 + m2).reshape(G, tm)

    g_iota = lax.broadcasted_iota(jnp.int32, (G, tm), 0).astype(F32)
    gsel = jnp.zeros((G, tm), jnp.bool_)
    cur = gs
    for _ in range(TOPK_GROUPS):
        m = jnp.max(cur, axis=0, keepdims=True)
        hit = g_iota == _first_index(cur == m, g_iota, float(G), 0)
        gsel = gsel | hit
        cur = jnp.where(hit, neg, cur)

    masked = jnp.where(gsel[:, None, :], g3, neg).reshape(E, tm)
    e_iota = lax.broadcasted_iota(jnp.int32, (E, tm), 0).astype(F32)
    sel = jnp.zeros((E, tm), jnp.bool_)
    hits, ids = [], []
    cur = masked
    for _ in range(TOP_K):
        m = jnp.max(cur, axis=0, keepdims=True)
        f = _first_index(cur == m, e_iota, float(E), 0)
        hit = e_iota == f
        hits.append(hit)
        ids.append(f)
        sel = sel | hit
        cur = jnp.where(hit, neg, cur)

    ssum = jnp.sum(jnp.where(sel, scores, 0.0), axis=0, keepdims=True)
    gates = scores / (ssum + 1e-20) * ROUTED_SCALE

    before = (lax.broadcasted_iota(jnp.int32, (tm, tm), 0) < lax.broadcasted_iota(jnp.int32, (tm, tm), 1))
    sel_f = sel.astype(F32)
    rank = _dot(sel_f.astype(BF16), before.astype(BF16)) + cnt_scr[:, 0:1]
    cnt_new = cnt_scr[...] + jnp.sum(sel_f, axis=1, keepdims=True)
    cnt_scr[...] = cnt_new
    cnt_ref[...] = cnt_new.astype(jnp.int32)

    k_iota = lax.broadcasted_iota(jnp.int32, (8, tm), 0)
    eidx8 = jnp.zeros((8, tm), F32)
    pos8 = jnp.zeros((8, tm), F32)
    gate8 = jnp.zeros((8, tm), F32)
    for k in range(TOP_K):
        row = k_iota == k
        eidx8 = jnp.where(row, ids[k], eidx8)
        pos8 = jnp.where(row, jnp.sum(jnp.where(hits[k], rank, 0.0), axis=0, keepdims=True), pos8)
        gate8 = jnp.where(row, jnp.sum(jnp.where(hits[k], gates, 0.0), axis=0, keepdims=True), gate8)
    eidx_ref[...] = eidx8.astype(jnp.int32)
    pos_ref[...] = pos8.astype(jnp.int32)
    gate_ref[...] = gate8


def route(h, w_router, router_bias):
    tt, dm = h.shape
    nt = tt // MOE_TILE
    tok = pl.BlockSpec((None, 8, MOE_TILE), lambda i: (i, 0, 0))
    eidx, pos, gate, cnt = pl.pallas_call(
        _router_kernel,
        out_shape=(jax.ShapeDtypeStruct((nt, 8, MOE_TILE), jnp.int32),
                   jax.ShapeDtypeStruct((nt, 8, MOE_TILE), jnp.int32),
                   jax.ShapeDtypeStruct((nt, 8, MOE_TILE), F32),
                   jax.ShapeDtypeStruct((N_EXPERTS, 128), jnp.int32)),
        grid=(nt,),
        in_specs=[pl.BlockSpec((MOE_TILE, dm), lambda i: (i, 0)),
                  pl.BlockSpec((N_EXPERTS, dm), lambda i: (0, 0)),
                  pl.BlockSpec((N_EXPERTS, 1), lambda i: (0, 0))],
        out_specs=(tok, tok, tok, pl.BlockSpec((N_EXPERTS, 128), lambda i: (0, 0))),
        scratch_shapes=[pltpu.VMEM((N_EXPERTS, 128), F32)],
        compiler_params=pltpu.CompilerParams(
            dimension_semantics=("arbitrary",), vmem_limit_bytes=VMEM_LIMIT_BYTES),
        name="moe_router",
    )(h, w_router.T, router_bias[:, None])
    return eidx, pos, gate, cnt[:, 0]


def _dispatch_kernel(start_ref, eidx_ref, pos_ref, x_ref, xs_ref, sem):
    tm = x_ref.shape[0]

    def issue(t, carry):
        for k in range(TOP_K):
            slot = start_ref[eidx_ref[k, t]] + pos_ref[k, t]
            pltpu.make_async_copy(x_ref.at[pl.ds(t, 1), :], xs_ref.at[pl.ds(slot, 1), :], sem).start()
        return carry

    lax.fori_loop(0, tm, issue, 0)
    for k in range(TOP_K):
        pltpu.make_async_copy(x_ref, xs_ref.at[pl.ds(0, tm), :], sem).wait()


def dispatch(h, start, eidx, pos):
    tt, dm = h.shape
    nt = tt // MOE_TILE
    smem_tok = pl.BlockSpec((None, 8, MOE_TILE), lambda i, st: (i, 0, 0), memory_space=pltpu.SMEM)
    return pl.pallas_call(
        _dispatch_kernel,
        out_shape=jax.ShapeDtypeStruct((tt * TOP_K, dm), F32),
        grid_spec=pltpu.PrefetchScalarGridSpec(
            num_scalar_prefetch=1,
            grid=(nt,),
            in_specs=[smem_tok, smem_tok, pl.BlockSpec((MOE_TILE, dm), lambda i, st: (i, 0))],
            out_specs=pl.BlockSpec(memory_space=pl.ANY),
            scratch_shapes=[pltpu.SemaphoreType.DMA]),
        compiler_params=pltpu.CompilerParams(
            dimension_semantics=("arbitrary",), vmem_limit_bytes=VMEM_LIMIT_BYTES),
        name="moe_dispatch",
    )(start, eidx, pos, h)


def _combine_kernel(start_ref, eidx_ref, pos_ref, gate_ref, base_ref, ys_ref, o_ref, buf, sem):
    tm = base_ref.shape[0]

    def issue(t, carry):
        for k in range(TOP_K):
            slot = start_ref[eidx_ref[k, t]] + pos_ref[k, t]
            pltpu.make_async_copy(ys_ref.at[pl.ds(slot, 1), :], buf.at[k, pl.ds(t, 1), :], sem).start()
        return carry

    lax.fori_loop(0, tm, issue, 0)
    for k in range(TOP_K):
        pltpu.make_async_copy(ys_ref.at[pl.ds(0, tm), :], buf.at[k], sem).wait()
    acc = base_ref[...]
    for k in range(TOP_K):
        acc = acc + gate_ref[:, k:k + 1] * buf[k]
    o_ref[...] = acc


def combine(base, ys, gate_cols, start, eidx, pos):
    tt, dm = base.shape
    nt = tt // MOE_TILE
    smem_tok = pl.BlockSpec((None, 8, MOE_TILE), lambda i, st: (i, 0, 0), memory_space=pltpu.SMEM)
    row = pl.BlockSpec((MOE_TILE, dm), lambda i, st: (i, 0))
    return pl.pallas_call(
        _combine_kernel,
        out_shape=jax.ShapeDtypeStruct((tt, dm), F32),
        grid_spec=pltpu.PrefetchScalarGridSpec(
            num_scalar_prefetch=1,
            grid=(nt,),
            in_specs=[smem_tok, smem_tok, pl.BlockSpec((MOE_TILE, 8), lambda i, st: (i, 0)), row,
                      pl.BlockSpec(memory_space=pl.ANY)],
            out_specs=row,
            scratch_shapes=[pltpu.VMEM((TOP_K, MOE_TILE, dm), F32), pltpu.SemaphoreType.DMA]),
        compiler_params=pltpu.CompilerParams(
            dimension_semantics=("arbitrary",), vmem_limit_bytes=VMEM_LIMIT_BYTES),
        name="moe_combine",
    )(start, eidx, pos, gate_cols, base, ys)


def _ffn_body(x, wg_ref, wu_ref, wd_ref, row_lo, row_hi):
    x = x.astype(BF16)
    h1 = _dot(x, wg_ref[...].astype(BF16))
    h2 = _dot(x, wu_ref[...].astype(BF16))
    h = (h1 * jax.nn.sigmoid(h1)) * h2
    if row_lo is not None:
        row = lax.broadcasted_iota(jnp.int32, h.shape, 0)
        h = jnp.where((row >= row_lo) & (row < row_hi), h, 0.0)
    return _dot(h.astype(BF16), wd_ref[...].astype(BF16))


def _ffn_items_kernel(tile_ref, exp_ref, lo_ref, hi_ref, first_ref, n_ref, x_ref, wg_ref, wu_ref, wd_ref, o_ref):
    del tile_ref, exp_ref
    w = pl.program_id(0)

    @pl.when(w < n_ref[0])
    def _():
        y = _ffn_body(x_ref[...], wg_ref, wu_ref, wd_ref, lo_ref[w], hi_ref[w])

        @pl.when(first_ref[w] == 1)
        def _():
            o_ref[...] = y

        @pl.when(first_ref[w] == 0)
        def _():
            o_ref[...] += y


def expert_ffn(xs, items, wg, wu, wd):
    rows, dm = xs.shape
    de = wg.shape[-1]
    n_items = items[0].shape[0]

    def row_map(w, tile, exp, lo, hi, first, n):
        return (tile[w], 0)

    def w_map(w, tile, exp, lo, hi, first, n):
        return (exp[w], 0, 0)

    return pl.pallas_call(
        _ffn_items_kernel,
        out_shape=jax.ShapeDtypeStruct((rows, dm), F32),
        grid_spec=pltpu.PrefetchScalarGridSpec(
            num_scalar_prefetch=6,
            grid=(n_items,),
            in_specs=[pl.BlockSpec((MOE_TILE, dm), row_map),
                      pl.BlockSpec((None, dm, de), w_map),
                      pl.BlockSpec((None, dm, de), w_map),
                      pl.BlockSpec((None, de, dm), w_map)],
            out_specs=pl.BlockSpec((MOE_TILE, dm), row_map)),
        compiler_params=pltpu.CompilerParams(
            dimension_semantics=("arbitrary",), vmem_limit_bytes=VMEM_LIMIT_BYTES),
        name="expert_ffn",
    )(*items, xs, wg, wu, wd)


def _ffn_dense_kernel(x_ref, wg_ref, wu_ref, wd_ref, o_ref):
    o_ref[...] = _ffn_body(x_ref[...], wg_ref, wu_ref, wd_ref, None, None)


def shared_ffn(x, wg, wu, wd):
    tt, dm = x.shape
    de = wg.shape[-1]
    row = pl.BlockSpec((MOE_TILE, dm), lambda i: (i, 0))
    return pl.pallas_call(
        _ffn_dense_kernel,
        out_shape=jax.ShapeDtypeStruct((tt, dm), F32),
        grid=(tt // MOE_TILE,),
        in_specs=[row, pl.BlockSpec((dm, de), lambda i: (0, 0)), pl.BlockSpec((dm, de), lambda i: (0, 0)),
                  pl.BlockSpec((de, dm), lambda i: (0, 0))],
        out_specs=row,
        compiler_params=pltpu.CompilerParams(
            dimension_semantics=("arbitrary",), vmem_limit_bytes=VMEM_LIMIT_BYTES),
        name="shared_ffn",
    )(x, wg, wu, wd)


def layer_norm(x, g, b):
    mu = jnp.mean(x, axis=-1, keepdims=True)
    var = jnp.mean(jnp.square(x - mu), axis=-1, keepdims=True)
    return (x - mu) * lax.rsqrt(var + LN_EPS) * g + b


def _ffn_items(counts, n_rows):
    n_tiles = n_rows // MOE_TILE
    n_items = n_tiles + N_EXPERTS - 1
    end = jnp.cumsum(counts)
    start = end - counts
    first_tile = start // MOE_TILE
    tiles_of = jnp.where(counts > 0, (end - 1) // MOE_TILE - first_tile + 1, 0)
    item_end = jnp.cumsum(tiles_of)
    item_start = item_end - tiles_of
    total = item_end[-1]
    w = jnp.minimum(jnp.arange(n_items, dtype=jnp.int32), total - 1)
    exp = jnp.sum((item_end[None, :] <= w[:, None]).astype(jnp.int32), axis=1)
    onehot = (exp[:, None] == jnp.arange(N_EXPERTS, dtype=jnp.int32)[None, :]).astype(jnp.int32)
    pick = lambda v: jnp.sum(onehot * v[None, :], axis=1)
    tile = pick(first_tile) + (w - pick(item_start))
    lo = jnp.maximum(pick(start) - tile * MOE_TILE, 0)
    hi = jnp.minimum(pick(end) - tile * MOE_TILE, MOE_TILE)
    prev_tile = jnp.concatenate([jnp.full((1,), -1, jnp.int32), tile[:-1]])
    first = (tile != prev_tile).astype(jnp.int32)
    i32 = lambda v: v.astype(jnp.int32)
    return start.astype(jnp.int32), (i32(tile), i32(exp), i32(lo), i32(hi), first, i32(total).reshape(1))


def moe(t, w_router, router_bias, w_gate, w_up, w_down, ws_gate, ws_up, ws_down):
    tt, dm = t.shape
    eidx, pos, gate, counts = route(t, w_router, router_bias)
    start, items = _ffn_items(counts, tt * TOP_K)
    xs = dispatch(t, start, eidx, pos)
    ys = expert_ffn(xs, items, w_gate, w_up, w_down)
    shared = shared_ffn(t, ws_gate, ws_up, ws_down)
    gate_cols = jnp.swapaxes(gate, 1, 2).reshape(tt, 8)
    return combine(shared, ys, gate_cols, start, eidx, pos)


def box_mean_1d(x, w, axis):
    n = x.shape[axis]
    lo_off = -(w // 2)
    acc = jnp.zeros_like(x)
    idx = jnp.arange(n)
    shape = [1] * x.ndim
    shape[axis] = n
    for off in range(lo_off, lo_off + w):
        shifted = jnp.roll(x, -off, axis=axis)
        valid = ((idx + off >= 0) & (idx + off < n)).reshape(shape)
        acc = acc + jnp.where(valid, shifted, 0.0)
    cnt = (jnp.minimum(idx + lo_off + w, n) - jnp.maximum(idx + lo_off, 0)).astype(F32).reshape(shape)
    return acc, cnt


def pool_pre(x, grid):
    b, l, dm = x.shape
    outs = []
    for gi, w in enumerate(POOL_WINDOWS):
        xg = x[..., gi * POOL_GROUP:(gi + 1) * POOL_GROUP]
        if grid:
            rows = l // GRID_W
            s, cr = box_mean_1d(xg.reshape(b, rows, GRID_W, POOL_GROUP), w, 1)
            s, cc = box_mean_1d(s, w, 2)
            mean = (s / (cr * cc)).reshape(b, l, POOL_GROUP)
        else:
            s, cnt = box_mean_1d(xg, w, 1)
            mean = s / cnt
        outs.append(mean - xg)
    return jnp.concatenate(outs, axis=-1)


def kernel(x_prompt, x_sample, c, state_rwkv, c_ctx, w_ada, b_ada, ln_g, ln_b, rw_mix_prev, rw_mix_next, rw_w_r, rw_w_k, rw_w_v, rw_w_o, rw_w0, rw_w1, rw_w2, rw_a0, rw_a1, rw_a2, rw_g1, rw_g2, rw_k_k, rw_k_a, rw_r_k, rw_gn_g, rw_gn_b, pool_w, pool_scale, moe_router, moe_router_bias, moe_w_gate, moe_w_up, moe_w_down, moe_ws_gate, moe_ws_up, moe_ws_down):
    dm = D_MODEL
    H, N = RWKV_HEADS, RWKV_HEAD
    bc, lc_, _ = x_prompt.shape
    bl, ll, _ = x_sample.shape
    tc = bc * lc_
    xs = [x_prompt, x_sample]
    cond = jnp.concatenate([c_ctx[None, :], c], axis=0)
    n_cond = cond.shape[0]
    cond_pad = jnp.zeros((8, dm), F32).at[:n_cond].set(jax.nn.silu(cond))
    new_states = []

    def flat(parts):
        return jnp.concatenate([p.reshape(-1, p.shape[-1]) for p in parts], axis=0)

    def unflat(t):
        return [t[:tc].reshape(bc, lc_, -1), t[tc:].reshape(bl, ll, -1)]

    for l in range(DEPTH):
        m = matmul(cond_pad, w_ada[l], tn=1024)[:n_cond] + b_ada[l]
        mods = jnp.split(m, 6, axis=-1)
        per = [[p[0:1][:, None, :] for p in mods], [p[1:][:, None, :] for p in mods]]
        hs = [xs[i] * (1.0 + per[i][1]) + per[i][0] for i in range(2)]
        j = l // 2
        if l % 2 == 0:
            mixes = []
            for h in hs:
                d_prev = jnp.pad(h[:, :-1], ((0, 0), (1, 0), (0, 0))) - h
                d_next = jnp.pad(h[:, 1:], ((0, 0), (0, 1), (0, 0))) - h
                mixes.append([(h + rw_mix_prev[j, i] * d_prev + rw_mix_next[j, i] * d_next).astype(BF16)
                              for i in range(6)])
            xr, xw, xk, xv, xa, xg = (flat([mixes[0][i], mixes[1][i]]) for i in range(6))
            r = matmul(xr, rw_w_r[j])
            k = matmul(xk, rw_w_k[j])
            v = matmul(xv, rw_w_v[j])
            g = matmul(jax.nn.sigmoid(matmul(xg, rw_g1[j])), rw_g2[j])
            wl = [matmul(jnp.tanh(matmul(xw, rw_w1[j, d])), rw_w2[j, d]) for d in range(2)]
            al = [matmul(matmul(xa, rw_a1[j, d]), rw_a2[j, d]) for d in range(2)]
            par = jnp.zeros((16, dm), F32).at[:9].set(jnp.stack(
                [rw_w0[j, 0], rw_w0[j, 1], rw_a0[j, 0], rw_a0[j, 1], rw_k_k[j], rw_k_a[j],
                 rw_r_k[j].reshape(dm), rw_gn_g[j], rw_gn_b[j]]))
            outs = []
            s0s = [jnp.zeros((bc, 2, H, N, N), F32), state_rwkv[:, j]]
            for i, (sl, bb, ln_) in enumerate([(slice(0, tc), bc, lc_), (slice(tc, None), bl, ll)]):
                seq = lambda t: t[sl].reshape(bb, ln_, dm)
                out, s_fin = rwkv_core(seq(r), seq(k), seq(v), seq(g), [seq(t) for t in wl],
                                       [seq(t) for t in al], par, s0s[i])
                outs.append(out.reshape(-1, dm))
                if i == 0:
                    new_states.append(s_fin)
            mix = unflat(matmul(jnp.concatenate(outs, axis=0), rw_w_o[j]))
        else:
            p = flat([pool_pre(hs[0], False), pool_pre(hs[1], True)]).astype(BF16)
            cols = [matmul(p[:, gi * POOL_GROUP:(gi + 1) * POOL_GROUP], pool_w[j, gi]) for gi in range(len(POOL_WINDOWS))]
            mix = unflat(jnp.concatenate(cols, axis=-1) * pool_scale[j])
        xs = [layer_norm(ALPHA * xs[i] + per[i][2] * mix[i], ln_g[l, 0], ln_b[l, 0]) for i in range(2)]
        h2 = flat([xs[i] * (1.0 + per[i][4]) + per[i][3] for i in range(2)])
        mo = unflat(moe(h2, moe_router[l], moe_router_bias[l], moe_w_gate[l], moe_w_up[l], moe_w_down[l],
                        moe_ws_gate[l], moe_ws_up[l], moe_ws_down[l]))
        xs = [layer_norm(ALPHA * xs[i] + per[i][5] * mo[i], ln_g[l, 1], ln_b[l, 1]) for i in range(2)]

    return (xs[0], xs[1], jnp.stack(new_states, axis=1))
```

```python
import functools

import jax
import jax.numpy as jnp
from jax import lax
from jax.experimental import pallas as pl
from jax.experimental.pallas import tpu as pltpu

D_MODEL = 2048
DEPTH = 2
GRID_W = 64
RWKV_HEAD = 64
RWKV_HEADS = D_MODEL // RWKV_HEAD
POOL_WINDOWS = (2, 4, 8, 16)
POOL_GROUP = D_MODEL // len(POOL_WINDOWS)
N_EXPERTS = 64
TOP_K = 6
N_GROUPS = 8
TOPK_GROUPS = 4
ROUTED_SCALE = 2.5
ALPHA = (2 * DEPTH) ** 0.25
LN_EPS = 1e-5
GN_EPS = 64e-5

F32 = jnp.float32
BF16 = jnp.bfloat16

VMEM_LIMIT_BYTES = 48 * 1024 * 1024

SCAN_CHUNK = 64
SCAN_HEADS = 4
SCAN_LANES = SCAN_HEADS * RWKV_HEAD
SCAN_PREP_CHUNKS = 4
MOE_TILE = 256


def _dot(a, b):
    return jnp.dot(a, b, preferred_element_type=F32)


def _dot_nt(a, b):
    return lax.dot_general(a, b, (((1,), (1,)), ((), ())), preferred_element_type=F32)


def _dot_tn(a, b):
    return lax.dot_general(a, b, (((0,), (0,)), ((), ())), preferred_element_type=F32)


def _mm_kernel(x_ref, w_ref, o_ref):
    o_ref[...] = _dot(x_ref[...].astype(BF16), w_ref[...].astype(BF16))


def matmul(x, w, *, tm=1024, tn=512):
    m, k = x.shape
    n = w.shape[1]
    tm = min(tm, m)
    tn = min(tn, n)
    assert m % tm == 0 and n % tn == 0
    return pl.pallas_call(
        _mm_kernel,
        out_shape=jax.ShapeDtypeStruct((m, n), F32),
        grid=(n // tn, m // tm),
        in_specs=[pl.BlockSpec((tm, k), lambda j, i: (i, 0)),
                  pl.BlockSpec((k, tn), lambda j, i: (0, j))],
        out_specs=pl.BlockSpec((tm, tn), lambda j, i: (i, j)),
        compiler_params=pltpu.CompilerParams(
            dimension_semantics=("arbitrary", "arbitrary"),
            vmem_limit_bytes=VMEM_LIMIT_BYTES),
        name="matmul",
    )(x, w)


def _split3(x):
    hi = x.astype(BF16)
    r1 = x - hi.astype(F32)
    mid = r1.astype(BF16)
    lo = (r1 - mid.astype(F32)).astype(BF16)
    return hi, mid, lo


def _softplus(z):
    return jnp.maximum(z, 0.0) + jnp.log(1.0 + jnp.exp(-jnp.abs(z)))


def _scan_kernel(r_ref, k_ref, v_ref, g_ref, wl0_ref, wl1_ref, al0_ref, al1_ref, par_ref, s0_ref,
                 out_ref, sfin_ref, s_scr, kk_ref, lw_ref, a_ref, kd_ref, bonus_ref, y_ref,
                 br_s, ak_s, inv_s, lra_s, wy2_s, gend_s, *, nc, prep):
    C, N, HB, W = SCAN_CHUNK, RWKV_HEAD, SCAN_HEADS, SCAN_LANES
    t_idx = lax.broadcasted_iota(jnp.int32, (C, W), 0)
    s_idx = lax.broadcasted_iota(jnp.int32, (C, W), 1) % C
    eye_row = (s_idx == t_idx).astype(F32)
    tri_t = lax.broadcasted_iota(jnp.int32, (C, C), 0)
    tri_s = lax.broadcasted_iota(jnp.int32, (C, C), 1)
    blk_r = lax.broadcasted_iota(jnp.int32, (W, W), 0) // N
    blk_c = lax.broadcasted_iota(jnp.int32, (W, W), 1) // N
    bd_mask = blk_r == blk_c
    bd_mask_bf = bd_mask.astype(BF16)

    def expand(x_row_bf):
        return jnp.concatenate([x_row_bf] * HB, axis=0) * bd_mask_bf

    for d in range(2):
        s_scr[d] = jnp.zeros((W, W), F32)
        for h in range(HB):
            s_scr[d, h * N:(h + 1) * N, h * N:(h + 1) * N] = s0_ref[d, h]
    y_ref[...] = jnp.zeros_like(y_ref)

    ones_bd = bd_mask_bf
    RB = 4 * C
    par = lambda i: par_ref[i:i + 1, :]

    def head_sum(x):
        hi, lo = _split2(x)
        return _dot(hi, ones_bd) + _dot(lo, ones_bd)

    def prologue(i, carry):
        rows = pl.ds(pl.multiple_of(i * RB, RB), RB)
        k = k_ref[rows, :]
        kkp = k * par(4)
        kk_ref[rows, :] = kkp * lax.rsqrt(jnp.maximum(head_sum(kkp * kkp), 1e-24))
        kd_sum = jnp.zeros_like(k)
        for d, (wl_ref, al_ref) in enumerate(((wl0_ref, al0_ref), (wl1_ref, al1_ref))):
            w_log = -_softplus(-(par(d) + wl_ref[rows, :])) - 0.5
            lw_ref[d, rows, :] = -jnp.exp(w_log)
            a = jax.nn.sigmoid(par(2 + d) + al_ref[rows, :])
            a_ref[d, rows, :] = a
            kd = k * (1.0 + (a - 1.0) * par(5))
            kd_ref[d, rows, :] = kd
            kd_sum = kd_sum + kd
        bonus_ref[rows, :] = head_sum(r_ref[rows, :] * kd_sum * par(6)) * v_ref[rows, :]
        return carry

    lax.fori_loop(0, nc * C // RB, prologue, 0)

    def interleave(chains):
        live = list(chains)
        while live:
            nxt = []
            for ch in live:
                try:
                    next(ch)
                    nxt.append(ch)
                except StopIteration:
                    pass
            live = nxt

    def prepare_chunk(d, ci):
        fwd = d == 0
        strict = (s_idx < t_idx) if fwd else (s_idx > t_idx)
        incl = (s_idx <= t_idx) if fwd else (s_idx >= t_idx)
        tri = ((tri_s <= tri_t) if fwd else (tri_s >= tri_t)).astype(BF16)
        rows = pl.ds(pl.multiple_of(ci * C, C), C)
        lw = lw_ref[d, rows, :]
        hi, mid, lo = _split3(lw)
        lc = _dot(tri, hi) + _dot(tri, mid) + _dot(tri, lo)
        yield
        kk = kk_ref[rows, :]
        g_in = jnp.exp(lc)
        g_inv = jnp.exp(-lc)
        g_ex = jnp.exp(lc - lw)
        last = C - 1 if fwd else 0
        gend_s[d, ci] = jnp.broadcast_to(g_in[last:last + 1, :], (8, W))
        bt = (kk * g_ex).astype(BF16)
        rt = (r_ref[rows, :] * g_in).astype(BF16)
        at = (-(kk * a_ref[d, rows, :]) * g_inv).astype(BF16)
        kt = (kd_ref[d, rows, :] * g_inv).astype(BF16)
        br = jnp.concatenate([bt, rt], axis=0)
        br_s[d, ci] = br
        ak_s[d, ci] = jnp.concatenate([at, kt], axis=0)
        la = _dot_nt(br, expand(at))
        lk = _dot_nt(br, expand(kt))
        yield
        lba = jnp.where(strict, la[:C], 0.0)
        lra_s[d, ci] = jnp.where(incl, la[C:], 0.0).astype(BF16)
        lbrk = jnp.concatenate([jnp.where(strict, lk[:C], 0.0), jnp.where(incl, lk[C:], 0.0)], axis=0)
        wy2_s[d, ci] = _dot(lbrk.astype(BF16), expand(v_ref[rows, :].astype(BF16)))

        def off(b):
            tb, sb = t_idx // b, s_idx // b
            if fwd:
                return jnp.where((tb % 2 == 1) & (sb == tb - 1), lba, 0.0)
            return jnp.where((tb % 2 == 0) & (sb == tb + 1), lba, 0.0)

        inv = eye_row + off(1)
        b = 2
        while b < C:
            m1 = _dot(inv.astype(BF16), expand(off(b).astype(BF16)))
            yield
            inv = inv + _dot(m1.astype(BF16), expand(inv.astype(BF16)))
            yield
            b *= 2
        inv_s[d, ci] = inv.astype(BF16)

    def prepare(i, carry):
        interleave([prepare_chunk(d, i * prep + j) for j in range(prep) for d in range(2)])
        return carry

    lax.fori_loop(0, nc // prep, prepare, 0)

    def advance_chunk(d, ci):
        rows = pl.ds(pl.multiple_of(ci * C, C), C)
        s_bd = s_scr[d]
        wy = _dot_nt(br_s[d, ci], s_bd.astype(BF16))
        yield
        wy2 = wy2_s[d, ci]
        w_row = wy[:C] + wy2[:C]
        u_bf = _dot(inv_s[d, ci], expand(w_row.astype(BF16))).astype(BF16)
        yield
        y = wy[C:] + wy2[C:] + _dot(lra_s[d, ci], expand(u_bf))
        upd = _dot_tn(jnp.concatenate([u_bf, v_ref[rows, :].astype(BF16)], axis=0), ak_s[d, ci])
        yield
        y_ref[rows, :] += y
        s_scr[d] = jnp.where(bd_mask, (s_bd + upd) * gend_s[d, ci][0:1, :], 0.0)

    def advance(c, carry):
        interleave([advance_chunk(0, c), advance_chunk(1, nc - 1 - c)])
        return carry

    lax.fori_loop(0, nc, advance, 0)

    for d in range(2):
        for h in range(HB):
            sfin_ref[d, h] = s_scr[d, h * N:(h + 1) * N, h * N:(h + 1) * N]

    def epilogue(i, carry):
        rows = pl.ds(pl.multiple_of(i * RB, RB), RB)
        y = y_ref[rows, :]
        yc = y - head_sum(y) * (1.0 / N)
        var = head_sum(yc * yc) * (1.0 / N)
        yn = yc * lax.rsqrt(var + GN_EPS) * par(7) + par(8)
        out_ref[rows, :] = ((yn + bonus_ref[rows, :]) * g_ref[rows, :]).astype(BF16)
        return carry

    lax.fori_loop(0, nc * C // RB, epilogue, 0)


def rwkv_core(r, k, v, g, wl, al, par, s0, first_row, seq_len):
    b = s0.shape[0]
    d = r.shape[1]
    l = seq_len
    assert first_row % l == 0
    nc = l // SCAN_CHUNK
    first_blk = first_row // l
    seq = pl.BlockSpec((l, SCAN_LANES), lambda i, j: (first_blk + i, j))
    oseq = pl.BlockSpec((l, SCAN_LANES), lambda i, j: (i, j))
    st = pl.BlockSpec((None, 2, SCAN_HEADS, RWKV_HEAD, RWKV_HEAD), lambda i, j: (i, 0, j, 0, 0))
    seq_scr = pltpu.VMEM((l, SCAN_LANES), F32)
    seq2_scr = pltpu.VMEM((2, l, SCAN_LANES), F32)
    C, W = SCAN_CHUNK, SCAN_LANES
    per_chunk = lambda rows, dt: pltpu.VMEM((2, nc, rows, W), dt)
    return pl.pallas_call(
        functools.partial(_scan_kernel, nc=nc, prep=SCAN_PREP_CHUNKS),
        out_shape=(jax.ShapeDtypeStruct((b * l, d), BF16), jax.ShapeDtypeStruct(s0.shape, F32)),
        grid=(b, d // SCAN_LANES),
        in_specs=[seq] * 8 + [pl.BlockSpec((16, SCAN_LANES), lambda i, j: (0, j)), st],
        out_specs=(oseq, st),
        scratch_shapes=[pltpu.VMEM((2, SCAN_LANES, SCAN_LANES), F32),
                        seq_scr, seq2_scr, seq2_scr, seq2_scr, seq_scr, seq_scr,
                        per_chunk(2 * C, BF16), per_chunk(2 * C, BF16), per_chunk(C, BF16),
                        per_chunk(C, BF16), per_chunk(2 * C, F32), per_chunk(8, F32)],
        compiler_params=pltpu.CompilerParams(
            dimension_semantics=("arbitrary", "arbitrary"),
            vmem_limit_bytes=VMEM_LIMIT_BYTES),
        name="wkv_scan",
    )(r, k, v, g, wl[0], wl[1], al[0], al[1], par, s0)


def _split2(x):
    hi = x.astype(BF16)
    return hi, (x - hi.astype(F32)).astype(BF16)


def _first_index(hit_value, iota, sentinel, axis):
    return jnp.min(jnp.where(hit_value, iota, sentinel), axis=axis, keepdims=True)


def _router_kernel(h_ref, wrt_ref, bias_ref, eidx_ref, pos_ref, gate_ref, cnt_ref, cnt_scr):
    tm = h_ref.shape[0]
    E, G, GS = N_EXPERTS, N_GROUPS, N_EXPERTS // N_GROUPS
    neg = -jnp.inf

    @pl.when(pl.program_id(0) == 0)
    def _():
        cnt_scr[...] = jnp.zeros_like(cnt_scr)

    x_hi, x_lo = _split2(h_ref[...])
    w_hi, w_lo = _split2(wrt_ref[...])
    logits = _dot_nt(w_hi, x_hi) + (_dot_nt(w_hi, x_lo) + _dot_nt(w_lo, x_hi))
    scores = jax.nn.sigmoid(logits)
    biased = scores + bias_ref[...]

    g3 = biased.reshape(G, GS, tm)
    w_iota = lax.broadcasted_iota(jnp.int32, (G, GS, tm), 1).astype(F32)
    m1 = jnp.max(g3, axis=1, keepdims=True)
    first = _first_index(g3 == m1, w_iota, float(GS), 1)
    m2 = jnp.max(jnp.where(w_iota == first, neg, g3), axis=1, keepdims=True)
    gs = (m1 + m2).reshape(G, tm)

    g_iota = lax.broadcasted_iota(jnp.int32, (G, tm), 0).astype(F32)
    gsel = jnp.zeros((G, tm), jnp.bool_)
    cur = gs
    for _ in range(TOPK_GROUPS):
        m = jnp.max(cur, axis=0, keepdims=True)
        hit = g_iota == _first_index(cur == m, g_iota, float(G), 0)
        gsel = gsel | hit
        cur = jnp.where(hit, neg, cur)

    masked = jnp.where(gsel[:, None, :], g3, neg).reshape(E, tm)
    e_iota = lax.broadcasted_iota(jnp.int32, (E, tm), 0).astype(F32)
    sel = jnp.zeros((E, tm), jnp.bool_)
    hits, ids = [], []
    cur = masked
    for _ in range(TOP_K):
        m = jnp.max(cur, axis=0, keepdims=True)
        f = _first_index(cur == m, e_iota, float(E), 0)
        hit = e_iota == f
        hits.append(hit)
        ids.append(f)
        sel = sel | hit
        cur = jnp.where(hit, neg, cur)

    ssum = jnp.sum(jnp.where(sel, scores, 0.0), axis=0, keepdims=True)
    gates = scores / (ssum + 1e-20) * ROUTED_SCALE

    before = (lax.broadcasted_iota(jnp.int32, (tm, tm), 0) < lax.broadcasted_iota(jnp.int32, (tm, tm), 1))
    sel_f = sel.astype(F32)
    rank = _dot(sel_f.astype(BF16), before.astype(BF16)) + cnt_scr[:, 0:1]
    cnt_new = cnt_scr[...] + jnp.sum(sel_f, axis=1, keepdims=True)
    cnt_scr[...] = cnt_new
    cnt_ref[...] = cnt_new.astype(jnp.int32)

    k_iota = lax.broadcasted_iota(jnp.int32, (8, tm), 0)
    eidx8 = jnp.zeros((8, tm), F32)
    pos8 = jnp.zeros((8, tm), F32)
    gate8 = jnp.zeros((8, tm), F32)
    for k in range(TOP_K):
        row = k_iota == k
        eidx8 = jnp.where(row, ids[k], eidx8)
        pos8 = jnp.where(row, jnp.sum(jnp.where(hits[k], rank, 0.0), axis=0, keepdims=True), pos8)
        gate8 = jnp.where(row, jnp.sum(jnp.where(hits[k], gates, 0.0), axis=0, keepdims=True), gate8)
    eidx_ref[...] = eidx8.astype(jnp.int32)
    pos_ref[...] = pos8.astype(jnp.int32)
    gate_ref[...] = gate8


def route(h, w_router, router_bias):
    tt, dm = h.shape
    nt = tt // MOE_TILE
    tok = pl.BlockSpec((None, 8, MOE_TILE), lambda i: (i, 0, 0))
    eidx, pos, gate, cnt = pl.pallas_call(
        _router_kernel,
        out_shape=(jax.ShapeDtypeStruct((nt, 8, MOE_TILE), jnp.int32),
                   jax.ShapeDtypeStruct((nt, 8, MOE_TILE), jnp.int32),
                   jax.ShapeDtypeStruct((nt, 8, MOE_TILE), F32),
                   jax.ShapeDtypeStruct((N_EXPERTS, 128), jnp.int32)),
        grid=(nt,),
        in_specs=[pl.BlockSpec((MOE_TILE, dm), lambda i: (i, 0)),
                  pl.BlockSpec((N_EXPERTS, dm), lambda i: (0, 0)),
                  pl.BlockSpec((N_EXPERTS, 1), lambda i: (0, 0))],
        out_specs=(tok, tok, tok, pl.BlockSpec((N_EXPERTS, 128), lambda i: (0, 0))),
        scratch_shapes=[pltpu.VMEM((N_EXPERTS, 128), F32)],
        compiler_params=pltpu.CompilerParams(
            dimension_semantics=("arbitrary",), vmem_limit_bytes=VMEM_LIMIT_BYTES),
        name="moe_router",
    )(h, w_router.T, router_bias[:, None])
    return eidx, pos, gate, cnt[:, 0]


def _dispatch_kernel(start_ref, eidx_ref, pos_ref, x_ref, xs_ref, sem):
    tm = x_ref.shape[0]

    def issue(t, carry):
        for k in range(TOP_K):
            slot = start_ref[eidx_ref[k, t]] + pos_ref[k, t]
            pltpu.make_async_copy(x_ref.at[pl.ds(t, 1), :], xs_ref.at[pl.ds(slot, 1), :], sem).start()
        return carry

    lax.fori_loop(0, tm, issue, 0)
    for k in range(TOP_K):
        pltpu.make_async_copy(x_ref, xs_ref.at[pl.ds(0, tm), :], sem).wait()


def dispatch(h, start, eidx, pos):
    tt, dm = h.shape
    nt = tt // MOE_TILE
    smem_tok = pl.BlockSpec((None, 8, MOE_TILE), lambda i, st: (i, 0, 0), memory_space=pltpu.SMEM)
    return pl.pallas_call(
        _dispatch_kernel,
        out_shape=jax.ShapeDtypeStruct((tt * TOP_K, dm), F32),
        grid_spec=pltpu.PrefetchScalarGridSpec(
            num_scalar_prefetch=1,
            grid=(nt,),
            in_specs=[smem_tok, smem_tok, pl.BlockSpec((MOE_TILE, dm), lambda i, st: (i, 0))],
            out_specs=pl.BlockSpec(memory_space=pl.ANY),
            scratch_shapes=[pltpu.SemaphoreType.DMA]),
        compiler_params=pltpu.CompilerParams(
            dimension_semantics=("arbitrary",), vmem_limit_bytes=VMEM_LIMIT_BYTES),
        name="moe_dispatch",
    )(start, eidx, pos, h)


def _combine_kernel(start_ref, eidx_ref, pos_ref, gate_ref, base_ref, ys_ref, o_ref, buf, sem):
    tm = base_ref.shape[0]

    def issue(t, carry):
        for k in range(TOP_K):
            slot = start_ref[eidx_ref[k, t]] + pos_ref[k, t]
            pltpu.make_async_copy(ys_ref.at[pl.ds(slot, 1), :], buf.at[k, pl.ds(t, 1), :], sem).start()
        return carry

    lax.fori_loop(0, tm, issue, 0)
    for k in range(TOP_K):
        pltpu.make_async_copy(ys_ref.at[pl.ds(0, tm), :], buf.at[k], sem).wait()
    acc = base_ref[...]
    for k in range(TOP_K):
        acc = acc + gate_ref[:, k:k + 1] * buf[k]
    o_ref[...] = acc


def combine(base, ys, gate_cols, start, eidx, pos):
    tt, dm = base.shape
    nt = tt // MOE_TILE
    smem_tok = pl.BlockSpec((None, 8, MOE_TILE), lambda i, st: (i, 0, 0), memory_space=pltpu.SMEM)
    row = pl.BlockSpec((MOE_TILE, dm), lambda i, st: (i, 0))
    return pl.pallas_call(
        _combine_kernel,
        out_shape=jax.ShapeDtypeStruct((tt, dm), F32),
        grid_spec=pltpu.PrefetchScalarGridSpec(
            num_scalar_prefetch=1,
            grid=(nt,),
            in_specs=[smem_tok, smem_tok, pl.BlockSpec((MOE_TILE, 8), lambda i, st: (i, 0)), row,
                      pl.BlockSpec(memory_space=pl.ANY)],
            out_specs=row,
            scratch_shapes=[pltpu.VMEM((TOP_K, MOE_TILE, dm), F32), pltpu.SemaphoreType.DMA]),
        compiler_params=pltpu.CompilerParams(
            dimension_semantics=("arbitrary",), vmem_limit_bytes=VMEM_LIMIT_BYTES),
        name="moe_combine",
    )(start, eidx, pos, gate_cols, base, ys)


def _ffn_body(x, wg_ref, wu_ref, wd_ref, row_lo, row_hi):
    x = x.astype(BF16)
    h1 = _dot(x, wg_ref[...].astype(BF16))
    h2 = _dot(x, wu_ref[...].astype(BF16))
    h = (h1 * jax.nn.sigmoid(h1)) * h2
    if row_lo is not None:
        row = lax.broadcasted_iota(jnp.int32, h.shape, 0)
        h = jnp.where((row >= row_lo) & (row < row_hi), h, 0.0)
    return _dot(h.astype(BF16), wd_ref[...].astype(BF16))


def _ffn_items_kernel(tile_ref, exp_ref, lo_ref, hi_ref, first_ref, n_ref, x_ref, wg_ref, wu_ref, wd_ref, o_ref):
    del tile_ref, exp_ref
    w = pl.program_id(0)

    @pl.when(w < n_ref[0])
    def _():
        y = _ffn_body(x_ref[...], wg_ref, wu_ref, wd_ref, lo_ref[w], hi_ref[w])

        @pl.when(first_ref[w] == 1)
        def _():
            o_ref[...] = y

        @pl.when(first_ref[w] == 0)
        def _():
            o_ref[...] += y


def expert_ffn(xs, items, wg, wu, wd):
    rows, dm = xs.shape
    de = wg.shape[-1]
    n_items = items[0].shape[0]

    def row_map(w, tile, exp, lo, hi, first, n):
        return (tile[w], 0)

    def w_map(w, tile, exp, lo, hi, first, n):
        return (exp[w], 0, 0)

    return pl.pallas_call(
        _ffn_items_kernel,
        out_shape=jax.ShapeDtypeStruct((rows, dm), F32),
        grid_spec=pltpu.PrefetchScalarGridSpec(
            num_scalar_prefetch=6,
            grid=(n_items,),
            in_specs=[pl.BlockSpec((MOE_TILE, dm), row_map),
                      pl.BlockSpec((None, dm, de), w_map),
                      pl.BlockSpec((None, dm, de), w_map),
                      pl.BlockSpec((None, de, dm), w_map)],
            out_specs=pl.BlockSpec((MOE_TILE, dm), row_map)),
        compiler_params=pltpu.CompilerParams(
            dimension_semantics=("arbitrary",), vmem_limit_bytes=VMEM_LIMIT_BYTES),
        name="expert_ffn",
    )(*items, xs, wg, wu, wd)


def _ffn_dense_kernel(x_ref, wg_ref, wu_ref, wd_ref, o_ref):
    o_ref[...] = _ffn_body(x_ref[...], wg_ref, wu_ref, wd_ref, None, None)


def shared_ffn(x, wg, wu, wd):
    tt, dm = x.shape
    de = wg.shape[-1]
    row = pl.BlockSpec((MOE_TILE, dm), lambda i: (i, 0))
    return pl.pallas_call(
        _ffn_dense_kernel,
        out_shape=jax.ShapeDtypeStruct((tt, dm), F32),
        grid=(tt // MOE_TILE,),
        in_specs=[row, pl.BlockSpec((dm, de), lambda i: (0, 0)), pl.BlockSpec((dm, de), lambda i: (0, 0)),
                  pl.BlockSpec((de, dm), lambda i: (0, 0))],
        out_specs=row,
        compiler_params=pltpu.CompilerParams(
            dimension_semantics=("arbitrary",), vmem_limit_bytes=VMEM_LIMIT_BYTES),
        name="shared_ffn",
    )(x, wg, wu, wd)


def layer_norm(x, g, b):
    mu = jnp.mean(x, axis=-1, keepdims=True)
    var = jnp.mean(jnp.square(x - mu), axis=-1, keepdims=True)
    return (x - mu) * lax.rsqrt(var + LN_EPS) * g + b


def _ffn_items(counts, n_rows):
    n_tiles = n_rows // MOE_TILE
    n_items = n_tiles + N_EXPERTS - 1
    end = jnp.cumsum(counts)
    start = end - counts
    first_tile = start // MOE_TILE
    tiles_of = jnp.where(counts > 0, (end - 1) // MOE_TILE - first_tile + 1, 0)
    item_end = jnp.cumsum(tiles_of)
    item_start = item_end - tiles_of
    total = item_end[-1]
    w = jnp.minimum(jnp.arange(n_items, dtype=jnp.int32), total - 1)
    exp = jnp.sum((item_end[None, :] <= w[:, None]).astype(jnp.int32), axis=1)
    onehot = (exp[:, None] == jnp.arange(N_EXPERTS, dtype=jnp.int32)[None, :]).astype(jnp.int32)
    pick = lambda v: jnp.sum(onehot * v[None, :], axis=1)
    tile = pick(first_tile) + (w - pick(item_start))
    lo = jnp.maximum(pick(start) - tile * MOE_TILE, 0)
    hi = jnp.minimum(pick(end) - tile * MOE_TILE, MOE_TILE)
    prev_tile = jnp.concatenate([jnp.full((1,), -1, jnp.int32), tile[:-1]])
    first = (tile != prev_tile).astype(jnp.int32)
    i32 = lambda v: v.astype(jnp.int32)
    return start.astype(jnp.int32), (i32(tile), i32(exp), i32(lo), i32(hi), first, i32(total).reshape(1))


def moe(t, w_router, router_bias, w_gate, w_up, w_down, ws_gate, ws_up, ws_down):
    tt, dm = t.shape
    eidx, pos, gate, counts = route(t, w_router, router_bias)
    start, items = _ffn_items(counts, tt * TOP_K)
    xs = dispatch(t, start, eidx, pos)
    ys = expert_ffn(xs, items, w_gate, w_up, w_down)
    shared = shared_ffn(t, ws_gate, ws_up, ws_down)
    gate_cols = jnp.swapaxes(gate, 1, 2).reshape(tt, 8)
    return combine(shared, ys, gate_cols, start, eidx, pos)


TOKEN_TILE = 256
MOD_SHIFT1, MOD_SCALE1, MOD_GATE1, MOD_SHIFT2, MOD_SCALE2, MOD_GATE2 = range(6)


class Layout:
    def __init__(self, n_ctx_seq, ctx_len, n_lat_seq, lat_len):
        self.n_ctx_seq, self.ctx_len, self.n_lat_seq, self.lat_len = n_ctx_seq, ctx_len, n_lat_seq, lat_len
        self.n_ctx = n_ctx_seq * ctx_len
        self.n_tok = self.n_ctx + n_lat_seq * lat_len
        assert ctx_len % TOKEN_TILE == 0 and lat_len % TOKEN_TILE == 0

    def cond_row(self, tile):
        ctx_tiles = self.n_ctx // TOKEN_TILE
        return jnp.where(tile < ctx_tiles, 0, 1 + (tile - ctx_tiles) // (self.lat_len // TOKEN_TILE))

    def tile_pos(self, tile):
        ctx_tiles = self.n_ctx // TOKEN_TILE
        per_ctx, per_lat = self.ctx_len // TOKEN_TILE, self.lat_len // TOKEN_TILE
        is_ctx = tile < ctx_tiles
        return (jnp.where(is_ctx, tile % per_ctx, (tile - ctx_tiles) % per_lat),
                jnp.where(is_ctx, per_ctx, per_lat))


def _premix_kernel(x_ref, prev_ref, next_ref, mod_ref, mp_ref, mn_ref, *out_refs, layout):
    tm = x_ref.shape[0]
    pos, per_seq = layout.tile_pos(pl.program_id(0))
    shift, scale = mod_ref[MOD_SHIFT1:MOD_SHIFT1 + 1, :], mod_ref[MOD_SCALE1:MOD_SCALE1 + 1, :]
    mod = lambda t: t * (1.0 + scale) + shift
    h = mod(x_ref[...])
    h_before = jnp.where(pos > 0, mod(prev_ref[...]), 0.0)
    h_after = jnp.where(pos < per_seq - 1, mod(next_ref[...]), 0.0)
    row = lax.broadcasted_iota(jnp.int32, h.shape, 0)
    d_prev = jnp.where(row == 0, h_before, pltpu.roll(h, 1, 0)) - h
    d_next = jnp.where(row == tm - 1, h_after, pltpu.roll(h, tm - 1, 0)) - h
    for i, o_ref in enumerate(out_refs):
        o_ref[...] = (h + mp_ref[i:i + 1, :] * d_prev + mn_ref[i:i + 1, :] * d_next).astype(BF16)


def premix(x, mods, mix_prev, mix_next, layout):
    tt, dm = x.shape
    nt = tt // TOKEN_TILE
    zero = jnp.zeros((1, dm), F32)
    prev_rows = jnp.concatenate([zero, x[TOKEN_TILE - 1::TOKEN_TILE][:-1]], axis=0)[:, None, :]
    next_rows = jnp.concatenate([x[::TOKEN_TILE][1:], zero], axis=0)[:, None, :]
    tile = pl.BlockSpec((TOKEN_TILE, dm), lambda i: (i, 0))
    edge = pl.BlockSpec((None, 1, dm), lambda i: (i, 0, 0))
    par = pl.BlockSpec((6, dm), lambda i: (0, 0))
    return pl.pallas_call(
        functools.partial(_premix_kernel, layout=layout),
        out_shape=[jax.ShapeDtypeStruct((tt, dm), BF16)] * 6,
        grid=(nt,),
        in_specs=[tile, edge, edge, pl.BlockSpec((None, 6, dm), lambda i: (layout.cond_row(i), 0, 0)), par, par],
        out_specs=[tile] * 6,
        compiler_params=pltpu.CompilerParams(
            dimension_semantics=("arbitrary",), vmem_limit_bytes=VMEM_LIMIT_BYTES),
        name="premix",
    )(x, prev_rows, next_rows, mods, mix_prev, mix_next)


def _post_kernel(x_ref, y_ref, mod_ref, g_ref, b_ref, *out_refs, gate_row, mod_rows):
    z = ALPHA * x_ref[...] + mod_ref[gate_row:gate_row + 1, :] * y_ref[...]
    mu = jnp.mean(z, axis=-1, keepdims=True)
    zc = z - mu
    var = jnp.mean(zc * zc, axis=-1, keepdims=True)
    xn = zc * lax.rsqrt(var + LN_EPS) * g_ref[...] + b_ref[...]
    out_refs[0][...] = xn
    if mod_rows is not None:
        sh, sc = mod_rows
        out_refs[1][...] = xn * (1.0 + mod_ref[sc:sc + 1, :]) + mod_ref[sh:sh + 1, :]


def post(x, y, mods, ln_g, ln_b, layout, gate_row, mod_rows):
    tt, dm = x.shape
    tile = pl.BlockSpec((TOKEN_TILE, dm), lambda i: (i, 0))
    vec = pl.BlockSpec((1, dm), lambda i: (0, 0))
    n_out = 1 if mod_rows is None else 2
    return pl.pallas_call(
        functools.partial(_post_kernel, gate_row=gate_row, mod_rows=mod_rows),
        out_shape=[jax.ShapeDtypeStruct((tt, dm), F32)] * n_out,
        grid=(tt // TOKEN_TILE,),
        in_specs=[tile, tile, pl.BlockSpec((None, 6, dm), lambda i: (layout.cond_row(i), 0, 0)), vec, vec],
        out_specs=[tile] * n_out,
        compiler_params=pltpu.CompilerParams(
            dimension_semantics=("arbitrary",), vmem_limit_bytes=VMEM_LIMIT_BYTES),
        name="post",
    )(x, y, mods, ln_g[None, :], ln_b[None, :])


def _pool_kernel(x_ref, mod_ref, w_ref, scale_ref, o_ref, *, grid_w):
    seq_len = x_ref.shape[0]
    t_idx = lax.broadcasted_iota(jnp.int32, x_ref.shape, 0)

    def shifted(v, k, pos, extent, stride):
        if k > 0:
            return jnp.where(pos < extent - k, pltpu.roll(v, seq_len - k * stride, 0), 0.0)
        return jnp.where(pos >= -k, pltpu.roll(v, -k * stride, 0), 0.0)

    def box(v, w, pos, extent, stride):
        m = w // 2
        ahead, behind, step = v, v, 1
        while step < m:
            ahead = ahead + shifted(ahead, step, pos, extent, stride)
            behind = behind + shifted(behind, -step, pos, extent, stride)
            step *= 2
        total = ahead + shifted(behind, -1, pos, extent, stride)
        cnt = jnp.minimum(pos + m, extent) - jnp.maximum(pos - m, 0)
        return total, cnt.astype(F32)

    h = x_ref[...] * (1.0 + mod_ref[MOD_SCALE1:MOD_SCALE1 + 1, :]) + mod_ref[MOD_SHIFT1:MOD_SHIFT1 + 1, :]
    for gi, w in enumerate(POOL_WINDOWS):
        @pl.when(pl.program_id(1) == gi)
        def _(w=w):
            if grid_w is None:
                s, cnt = box(h, w, t_idx, seq_len, 1)
            else:
                s, cr = box(h, w, t_idx // grid_w, seq_len // grid_w, grid_w)
                s, cc = box(s, w, t_idx % grid_w, grid_w, 1)
                cnt = cr * cc
            p = (s / cnt - h).astype(BF16)
            o_ref[...] = _dot(p, w_ref[...].astype(BF16)) * scale_ref[...]


def pool_mix(x, mods, pool_w, pool_scale, layout, latent):
    dm = x.shape[1]
    n_grp = len(POOL_WINDOWS)
    if latent:
        n_seq, seq_len, first_blk, cond0 = layout.n_lat_seq, layout.lat_len, layout.n_ctx // layout.lat_len, 1
    else:
        n_seq, seq_len, first_blk, cond0 = layout.n_ctx_seq, layout.ctx_len, 0, None
    cond = (lambda i: 0) if cond0 is None else (lambda i: cond0 + i)
    return pl.pallas_call(
        functools.partial(_pool_kernel, grid_w=GRID_W if latent else None),
        out_shape=jax.ShapeDtypeStruct((n_seq * seq_len, dm), F32),
        grid=(n_seq, n_grp),
        in_specs=[pl.BlockSpec((seq_len, POOL_GROUP), lambda i, g: (first_blk + i, g)),
                  pl.BlockSpec((None, 6, POOL_GROUP), lambda i, g: (cond(i), 0, g)),
                  pl.BlockSpec((None, POOL_GROUP, POOL_GROUP), lambda i, g: (g, 0, 0)),
                  pl.BlockSpec((1, POOL_GROUP), lambda i, g: (0, g))],
        out_specs=pl.BlockSpec((seq_len, POOL_GROUP), lambda i, g: (i, g)),
        compiler_params=pltpu.CompilerParams(
            dimension_semantics=("arbitrary", "arbitrary"), vmem_limit_bytes=VMEM_LIMIT_BYTES),
        name="pool_mix",
    )(x, mods, pool_w, pool_scale[None, :])


def kernel(x_prompt, x_sample, c, state_rwkv, c_ctx, w_ada, b_ada, ln_g, ln_b, rw_mix_prev, rw_mix_next, rw_w_r, rw_w_k, rw_w_v, rw_w_o, rw_w0, rw_w1, rw_w2, rw_a0, rw_a1, rw_a2, rw_g1, rw_g2, rw_k_k, rw_k_a, rw_r_k, rw_gn_g, rw_gn_b, pool_w, pool_scale, moe_router, moe_router_bias, moe_w_gate, moe_w_up, moe_w_down, moe_ws_gate, moe_ws_up, moe_ws_down):
    dm = D_MODEL
    H, N = RWKV_HEADS, RWKV_HEAD
    bc, lc_, _ = x_prompt.shape
    bl, ll, _ = x_sample.shape
    layout = Layout(bc, lc_, bl, ll)
    tc = layout.n_ctx
    x = jnp.concatenate([x_prompt.reshape(tc, dm), x_sample.reshape(bl * ll, dm)], axis=0)
    cond = jnp.concatenate([c_ctx[None, :], c], axis=0)
    n_cond = cond.shape[0]
    assert n_cond <= 8
    cond_pad = jnp.zeros((8, dm), F32).at[:n_cond].set(jax.nn.silu(cond))
    new_states = []

    for l in range(DEPTH):
        mods = (matmul(cond_pad, w_ada[l], tn=1024) + b_ada[l]).reshape(8, 6, dm)
        j = l // 2
        if l % 2 == 0:
            xr, xw, xk, xv, xa, xg = premix(x, mods, rw_mix_prev[j], rw_mix_next[j], layout)
            r = matmul(xr, rw_w_r[j])
            k = matmul(xk, rw_w_k[j])
            v = matmul(xv, rw_w_v[j])
            g = matmul(jax.nn.sigmoid(matmul(xg, rw_g1[j])), rw_g2[j])
            wl = [matmul(jnp.tanh(matmul(xw, rw_w1[j, d])), rw_w2[j, d]) for d in range(2)]
            al = [matmul(matmul(xa, rw_a1[j, d]), rw_a2[j, d]) for d in range(2)]
            par = jnp.zeros((16, dm), F32).at[:9].set(jnp.stack(
                [rw_w0[j, 0], rw_w0[j, 1], rw_a0[j, 0], rw_a0[j, 1], rw_k_k[j], rw_k_a[j],
                 rw_r_k[j].reshape(dm), rw_gn_g[j], rw_gn_b[j]]))
            out_c, s_fin = rwkv_core(r, k, v, g, wl, al, par, jnp.zeros((bc, 2, H, N, N), F32), 0, lc_)
            out_l, _ = rwkv_core(r, k, v, g, wl, al, par, state_rwkv[:, j], tc, ll)
            new_states.append(s_fin)
            mix = matmul(jnp.concatenate([out_c, out_l], axis=0), rw_w_o[j])
        else:
            mix = jnp.concatenate([pool_mix(x, mods, pool_w[j], pool_scale[j], layout, False),
                                   pool_mix(x, mods, pool_w[j], pool_scale[j], layout, True)], axis=0)
        x, h2 = post(x, mix, mods, ln_g[l, 0], ln_b[l, 0], layout, MOD_GATE1, (MOD_SHIFT2, MOD_SCALE2))
        mo = moe(h2, moe_router[l], moe_router_bias[l], moe_w_gate[l], moe_w_up[l], moe_w_down[l],
                 moe_ws_gate[l], moe_ws_up[l], moe_ws_down[l])
        x, = post(x, mo, mods, ln_g[l, 1], ln_b[l, 1], layout, MOD_GATE2, None)

    return (x[:tc].reshape(bc, lc_, dm), x[tc:].reshape(bl, ll, dm), jnp.stack(new_states, axis=1))
```

```python
import functools

import jax
import jax.numpy as jnp
from jax import lax
from jax.experimental import pallas as pl
from jax.experimental.pallas import tpu as pltpu

D_MODEL = 2048
DEPTH = 2
GRID_W = 64
RWKV_HEAD = 64
RWKV_HEADS = D_MODEL // RWKV_HEAD
POOL_WINDOWS = (2, 4, 8, 16)
POOL_GROUP = D_MODEL // len(POOL_WINDOWS)
N_EXPERTS = 64
TOP_K = 6
N_GROUPS = 8
TOPK_GROUPS = 4
ROUTED_SCALE = 2.5
ALPHA = (2 * DEPTH) ** 0.25
LN_EPS = 1e-5
GN_EPS = 64e-5

F32 = jnp.float32
BF16 = jnp.bfloat16

VMEM_LIMIT_BYTES = 48 * 1024 * 1024

SCAN_CHUNK = 64
SCAN_HEADS = 4
SCAN_LANES = SCAN_HEADS * RWKV_HEAD
SCAN_PREP_CHUNKS = 4
MOE_TILE = 256


def _dot(a, b):
    return jnp.dot(a, b, preferred_element_type=F32)


def _dot_nt(a, b):
    return lax.dot_general(a, b, (((1,), (1,)), ((), ())), preferred_element_type=F32)


def _dot_tn(a, b):
    return lax.dot_general(a, b, (((0,), (0,)), ((), ())), preferred_element_type=F32)


def _mm_kernel(x_ref, w_ref, o_ref):
    o_ref[...] = _dot(x_ref[...].astype(BF16), w_ref[...].astype(BF16))


def matmul(x, w, *, tm=1024, tn=512, layer=None):
    m, k = x.shape
    n = w.shape[-1]
    tm = min(tm, m)
    tn = min(tn, n)
    assert m % tm == 0 and n % tn == 0
    if layer is None:
        w_spec = pl.BlockSpec((k, tn), lambda j, i: (0, j))
    else:
        w_spec = pl.BlockSpec((None, k, tn), lambda j, i: (layer, 0, j))
    return pl.pallas_call(
        _mm_kernel,
        out_shape=jax.ShapeDtypeStruct((m, n), F32),
        grid=(n // tn, m // tm),
        in_specs=[pl.BlockSpec((tm, k), lambda j, i: (i, 0)), w_spec],
        out_specs=pl.BlockSpec((tm, tn), lambda j, i: (i, j)),
        compiler_params=pltpu.CompilerParams(
            dimension_semantics=("arbitrary", "arbitrary"),
            vmem_limit_bytes=VMEM_LIMIT_BYTES),
        name="matmul",
    )(x, w)


def _split3(x):
    hi = x.astype(BF16)
    r1 = x - hi.astype(F32)
    mid = r1.astype(BF16)
    lo = (r1 - mid.astype(F32)).astype(BF16)
    return hi, mid, lo


def _softplus(z):
    return jnp.maximum(z, 0.0) + jnp.log(1.0 + jnp.exp(-jnp.abs(z)))


def _scan_kernel(r_ref, k_ref, v_ref, g_ref, wl0_ref, wl1_ref, al0_ref, al1_ref, par_ref, s0_ref,
                 out_ref, sfin_ref, s_scr, kk_ref, lw_ref, a_ref, kd_ref, bonus_ref, y_ref,
                 pr_s, ak_s, lra_s, uy_s, gend_s, lvl_s, *, nc, prep):
    C, N, HB, W = SCAN_CHUNK, RWKV_HEAD, SCAN_HEADS, SCAN_LANES
    t_idx = lax.broadcasted_iota(jnp.int32, (C, W), 0)
    s_idx = lax.broadcasted_iota(jnp.int32, (C, W), 1) % C
    eye_row = (s_idx == t_idx).astype(F32)
    tri_t = lax.broadcasted_iota(jnp.int32, (C, C), 0)
    tri_s = lax.broadcasted_iota(jnp.int32, (C, C), 1)
    blk_r = lax.broadcasted_iota(jnp.int32, (W, W), 0) // N
    blk_c = lax.broadcasted_iota(jnp.int32, (W, W), 1) // N
    bd_mask = blk_r == blk_c
    bd_mask_bf = bd_mask.astype(BF16)

    def expand(x_row_bf):
        return jnp.concatenate([x_row_bf] * HB, axis=0) * bd_mask_bf

    for d in range(2):
        s_scr[d] = jnp.zeros((W, W), F32)
        for h in range(HB):
            s_scr[d, h * N:(h + 1) * N, h * N:(h + 1) * N] = s0_ref[d, h]
    y_ref[...] = jnp.zeros_like(y_ref)

    levels = []
    b = 2
    while b < C:
        levels.append(b)
        b *= 2

    @pl.when((pl.program_id(0) == 0) & (pl.program_id(1) == 0))
    def _():
        t_bd = lax.broadcasted_iota(jnp.int32, (W, W), 0) % C
        s_bd = lax.broadcasted_iota(jnp.int32, (W, W), 1) % C
        for li, b in enumerate(levels):
            tb, sb = t_bd // b, s_bd // b
            lvl_s[0, li] = (bd_mask & (tb % 2 == 1) & (sb == tb - 1)).astype(BF16)
            lvl_s[1, li] = (bd_mask & (tb % 2 == 0) & (sb == tb + 1)).astype(BF16)

    ones_bd = bd_mask_bf
    RB = 4 * C
    par = lambda i: par_ref[i:i + 1, :]

    def head_sum(x):
        hi, lo = _split2(x)
        return _dot(hi, ones_bd) + _dot(lo, ones_bd)

    def prologue(i, carry):
        rows = pl.ds(pl.multiple_of(i * RB, RB), RB)
        k = k_ref[rows, :]
        kkp = k * par(4)
        kk_ref[rows, :] = kkp * lax.rsqrt(jnp.maximum(head_sum(kkp * kkp), 1e-24))
        kd_sum = jnp.zeros_like(k)
        for d, (wl_ref, al_ref) in enumerate(((wl0_ref, al0_ref), (wl1_ref, al1_ref))):
            w_log = -_softplus(-(par(d) + wl_ref[rows, :])) - 0.5
            lw_ref[d, rows, :] = -jnp.exp(w_log)
            a = jax.nn.sigmoid(par(2 + d) + al_ref[rows, :])
            a_ref[d, rows, :] = a
            kd = k * (1.0 + (a - 1.0) * par(5))
            kd_ref[d, rows, :] = kd
            kd_sum = kd_sum + kd
        bonus_ref[rows, :] = head_sum(r_ref[rows, :] * kd_sum * par(6)) * v_ref[rows, :]
        return carry

    lax.fori_loop(0, nc * C // RB, prologue, 0)

    def interleave(chains):
        live = list(chains)
        while live:
            nxt = []
            for ch in live:
                try:
                    next(ch)
                    nxt.append(ch)
                except StopIteration:
                    pass
            live = nxt

    def prepare_chunk(d, ci):
        fwd = d == 0
        strict = (s_idx < t_idx) if fwd else (s_idx > t_idx)
        incl = (s_idx <= t_idx) if fwd else (s_idx >= t_idx)
        tri = ((tri_s <= tri_t) if fwd else (tri_s >= tri_t)).astype(BF16)
        rows = pl.ds(pl.multiple_of(ci * C, C), C)
        lw = lw_ref[d, rows, :]
        hi, mid, lo = _split3(lw)
        lc = _dot(tri, hi) + _dot(tri, mid) + _dot(tri, lo)
        yield
        kk = kk_ref[rows, :]
        g_in = jnp.exp(lc)
        g_inv = jnp.exp(-lc)
        g_ex = jnp.exp(lc - lw)
        last = C - 1 if fwd else 0
        gend_s[d, ci] = jnp.broadcast_to(g_in[last:last + 1, :], (8, W))
        bt = (kk * g_ex).astype(BF16)
        rt = (r_ref[rows, :] * g_in).astype(BF16)
        at = (-(kk * a_ref[d, rows, :]) * g_inv).astype(BF16)
        kt = (kd_ref[d, rows, :] * g_inv).astype(BF16)
        br = jnp.concatenate([bt, rt], axis=0)
        ak_s[d, ci] = jnp.concatenate([at, kt], axis=0)
        la = _dot_nt(br, expand(at))
        lk = _dot_nt(br, expand(kt))
        yield
        lba = la[:C]
        lra_s[d, ci] = jnp.where(incl, la[C:], 0.0).astype(BF16)
        lbrk = jnp.concatenate([jnp.where(strict, lk[:C], 0.0), jnp.where(incl, lk[C:], 0.0)], axis=0)
        wy2 = _dot(lbrk.astype(BF16), expand(v_ref[rows, :].astype(BF16)))

        pair = (s_idx == t_idx - 1) if fwd else (s_idx == t_idx + 1)
        inv = eye_row + jnp.where(pair & (t_idx % 2 == (1 if fwd else 0)), lba, 0.0)
        lba4 = jnp.concatenate([lba.astype(BF16)] * HB, axis=0)
        for li in range(len(levels)):
            m1 = _dot(inv.astype(BF16), lba4 * lvl_s[d, li])
            yield
            inv = inv + _dot(m1.astype(BF16), expand(inv.astype(BF16)))
            yield
        inv_bf = inv.astype(BF16)
        pu = _dot(inv_bf, expand(bt))
        u0 = _dot(inv_bf, expand(wy2[:C].astype(BF16)))
        yield
        pr_s[d, ci] = jnp.concatenate([pu.astype(BF16), rt], axis=0)
        uy_s[d, ci] = jnp.concatenate([u0, wy2[C:]], axis=0)

    def prepare(i, carry):
        interleave([prepare_chunk(d, i * prep + j) for j in range(prep) for d in range(2)])
        return carry

    lax.fori_loop(0, nc // prep, prepare, 0)

    def advance_chunk(d, ci):
        rows = pl.ds(pl.multiple_of(ci * C, C), C)
        s_bd = s_scr[d]
        py = _dot_nt(pr_s[d, ci], s_bd.astype(BF16))
        yield
        uy = uy_s[d, ci]
        u_bf = (py[:C] + uy[:C]).astype(BF16)
        upd = _dot_tn(jnp.concatenate([u_bf, v_ref[rows, :].astype(BF16)], axis=0), ak_s[d, ci])
        y = py[C:] + uy[C:] + _dot(lra_s[d, ci], expand(u_bf))
        yield
        y_ref[rows, :] += y
        s_scr[d] = jnp.where(bd_mask, (s_bd + upd) * gend_s[d, ci][0:1, :], 0.0)

    def advance(c, carry):
        interleave([advance_chunk(0, c), advance_chunk(1, nc - 1 - c)])
        return carry

    lax.fori_loop(0, nc, advance, 0)

    for d in range(2):
        for h in range(HB):
            sfin_ref[d, h] = s_scr[d, h * N:(h + 1) * N, h * N:(h + 1) * N]

    def epilogue(i, carry):
        rows = pl.ds(pl.multiple_of(i * RB, RB), RB)
        y = y_ref[rows, :]
        yc = y - head_sum(y) * (1.0 / N)
        var = head_sum(yc * yc) * (1.0 / N)
        yn = yc * lax.rsqrt(var + GN_EPS) * par(7) + par(8)
        out_ref[rows, :] = ((yn + bonus_ref[rows, :]) * g_ref[rows, :]).astype(BF16)
        return carry

    lax.fori_loop(0, nc * C // RB, epilogue, 0)


def rwkv_core(r, k, v, g, wl, al, par, s0, first_row, seq_len):
    b = s0.shape[0]
    d = r.shape[1]
    l = seq_len
    assert first_row % l == 0
    nc = l // SCAN_CHUNK
    first_blk = first_row // l
    seq = pl.BlockSpec((l, SCAN_LANES), lambda i, j: (first_blk + i, j))
    oseq = pl.BlockSpec((l, SCAN_LANES), lambda i, j: (i, j))
    st = pl.BlockSpec((None, 2, SCAN_HEADS, RWKV_HEAD, RWKV_HEAD), lambda i, j: (i, 0, j, 0, 0))
    seq_scr = pltpu.VMEM((l, SCAN_LANES), F32)
    seq2_scr = pltpu.VMEM((2, l, SCAN_LANES), F32)
    C, W = SCAN_CHUNK, SCAN_LANES
    per_chunk = lambda rows, dt: pltpu.VMEM((2, nc, rows, W), dt)
    return pl.pallas_call(
        functools.partial(_scan_kernel, nc=nc, prep=SCAN_PREP_CHUNKS),
        out_shape=(jax.ShapeDtypeStruct((b * l, d), BF16), jax.ShapeDtypeStruct(s0.shape, F32)),
        grid=(b, d // SCAN_LANES),
        in_specs=[seq] * 8 + [pl.BlockSpec((16, SCAN_LANES), lambda i, j: (0, j)), st],
        out_specs=(oseq, st),
        scratch_shapes=[pltpu.VMEM((2, SCAN_LANES, SCAN_LANES), F32),
                        seq_scr, seq2_scr, seq2_scr, seq2_scr, seq_scr, seq_scr,
                        per_chunk(2 * C, BF16), per_chunk(2 * C, BF16), per_chunk(C, BF16),
                        per_chunk(2 * C, F32), per_chunk(8, F32),
                        pltpu.VMEM((2, 5, W, W), BF16)],
        compiler_params=pltpu.CompilerParams(
            dimension_semantics=("arbitrary", "arbitrary"),
            vmem_limit_bytes=VMEM_LIMIT_BYTES),
        name="wkv_scan",
    )(r, k, v, g, wl[0], wl[1], al[0], al[1], par, s0)


def _split2(x):
    hi = x.astype(BF16)
    return hi, (x - hi.astype(F32)).astype(BF16)


def _first_index(hit_value, iota, sentinel, axis):
    return jnp.min(jnp.where(hit_value, iota, sentinel), axis=axis, keepdims=True)


def _router_kernel(h_ref, wrt_ref, bias_ref, eidx_ref, pos_ref, gate_ref, cnt_ref, cnt_scr):
    tm = h_ref.shape[0]
    E, G, GS = N_EXPERTS, N_GROUPS, N_EXPERTS // N_GROUPS
    neg = -jnp.inf

    @pl.when(pl.program_id(0) == 0)
    def _():
        cnt_scr[...] = jnp.zeros_like(cnt_scr)

    x_hi, x_lo = _split2(h_ref[...])
    w_hi, w_lo = _split2(wrt_ref[...])
    logits = _dot_nt(w_hi, x_hi) + (_dot_nt(w_hi, x_lo) + _dot_nt(w_lo, x_hi))
    scores = jax.nn.sigmoid(logits)
    biased = scores + bias_ref[...]

    g3 = biased.reshape(G, GS, tm)
    w_iota = lax.broadcasted_iota(jnp.int32, (G, GS, tm), 1).astype(F32)
    m1 = jnp.max(g3, axis=1, keepdims=True)
    first = _first_index(g3 == m1, w_iota, float(GS), 1)
    m2 = jnp.max(jnp.where(w_iota == first, neg, g3), axis=1, keepdims=True)
    gs = (m1 + m2).reshape(G, tm)

    g_iota = lax.broadcasted_iota(jnp.int32, (G, tm), 0).astype(F32)
    gsel = jnp.zeros((G, tm), jnp.bool_)
    cur = gs
    for _ in range(TOPK_GROUPS):
        m = jnp.max(cur, axis=0, keepdims=True)
        hit = g_iota == _first_index(cur == m, g_iota, float(G), 0)
        gsel = gsel | hit
        cur = jnp.where(hit, neg, cur)

    masked = jnp.where(gsel[:, None, :], g3, neg).reshape(E, tm)
    e_iota = lax.broadcasted_iota(jnp.int32, (E, tm), 0).astype(F32)
    sel = jnp.zeros((E, tm), jnp.bool_)
    hits, ids = [], []
    cur = masked
    for _ in range(TOP_K):
        m = jnp.max(cur, axis=0, keepdims=True)
        f = _first_index(cur == m, e_iota, float(E), 0)
        hit = e_iota == f
        hits.append(hit)
        ids.append(f)
        sel = sel | hit
        cur = jnp.where(hit, neg, cur)

    ssum = jnp.sum(jnp.where(sel, scores, 0.0), axis=0, keepdims=True)
    gates = scores / (ssum + 1e-20) * ROUTED_SCALE

    before = (lax.broadcasted_iota(jnp.int32, (tm, tm), 0) < lax.broadcasted_iota(jnp.int32, (tm, tm), 1))
    sel_f = sel.astype(F32)
    rank = _dot(sel_f.astype(BF16), before.astype(BF16)) + cnt_scr[:, 0:1]
    cnt_new = cnt_scr[...] + jnp.sum(sel_f, axis=1, keepdims=True)
    cnt_scr[...] = cnt_new
    cnt_ref[...] = cnt_new.astype(jnp.int32)

    k_iota = lax.broadcasted_iota(jnp.int32, (8, tm), 0)
    eidx8 = jnp.zeros((8, tm), F32)
    pos8 = jnp.zeros((8, tm), F32)
    gate8 = jnp.zeros((8, tm), F32)
    for k in range(TOP_K):
        row = k_iota == k
        eidx8 = jnp.where(row, ids[k], eidx8)
        pos8 = jnp.where(row, jnp.sum(jnp.where(hits[k], rank, 0.0), axis=0, keepdims=True), pos8)
        gate8 = jnp.where(row, jnp.sum(jnp.where(hits[k], gates, 0.0), axis=0, keepdims=True), gate8)
    eidx_ref[...] = eidx8.astype(jnp.int32)
    pos_ref[...] = pos8.astype(jnp.int32)
    gate_ref[...] = gate8


def route(h, w_router, router_bias):
    tt, dm = h.shape
    nt = tt // MOE_TILE
    tok = pl.BlockSpec((None, 8, MOE_TILE), lambda i: (i, 0, 0))
    eidx, pos, gate, cnt = pl.pallas_call(
        _router_kernel,
        out_shape=(jax.ShapeDtypeStruct((nt, 8, MOE_TILE), jnp.int32),
                   jax.ShapeDtypeStruct((nt, 8, MOE_TILE), jnp.int32),
                   jax.ShapeDtypeStruct((nt, 8, MOE_TILE), F32),
                   jax.ShapeDtypeStruct((N_EXPERTS, 128), jnp.int32)),
        grid=(nt,),
        in_specs=[pl.BlockSpec((MOE_TILE, dm), lambda i: (i, 0)),
                  pl.BlockSpec((N_EXPERTS, dm), lambda i: (0, 0)),
                  pl.BlockSpec((N_EXPERTS, 1), lambda i: (0, 0))],
        out_specs=(tok, tok, tok, pl.BlockSpec((N_EXPERTS, 128), lambda i: (0, 0))),
        scratch_shapes=[pltpu.VMEM((N_EXPERTS, 128), F32)],
        compiler_params=pltpu.CompilerParams(
            dimension_semantics=("arbitrary",), vmem_limit_bytes=VMEM_LIMIT_BYTES),
        name="moe_router",
    )(h, w_router.T, router_bias[:, None])
    return eidx, pos, gate, cnt[:, 0]


def _dispatch_kernel(start_ref, eidx_ref, pos_ref, x_ref, xs_ref, sem):
    tm = x_ref.shape[0]

    def issue(t, carry):
        for k in range(TOP_K):
            slot = start_ref[eidx_ref[k, t]] + pos_ref[k, t]
            pltpu.make_async_copy(x_ref.at[pl.ds(t, 1), :], xs_ref.at[pl.ds(slot, 1), :], sem).start()
        return carry

    lax.fori_loop(0, tm, issue, 0)
    for k in range(TOP_K):
        pltpu.make_async_copy(x_ref, xs_ref.at[pl.ds(0, tm), :], sem).wait()


def dispatch(h, start, eidx, pos):
    tt, dm = h.shape
    nt = tt // MOE_TILE
    smem_tok = pl.BlockSpec((None, 8, MOE_TILE), lambda i, st: (i, 0, 0), memory_space=pltpu.SMEM)
    return pl.pallas_call(
        _dispatch_kernel,
        out_shape=jax.ShapeDtypeStruct((tt * TOP_K, dm), F32),
        grid_spec=pltpu.PrefetchScalarGridSpec(
            num_scalar_prefetch=1,
            grid=(nt,),
            in_specs=[smem_tok, smem_tok, pl.BlockSpec((MOE_TILE, dm), lambda i, st: (i, 0))],
            out_specs=pl.BlockSpec(memory_space=pl.ANY),
            scratch_shapes=[pltpu.SemaphoreType.DMA]),
        compiler_params=pltpu.CompilerParams(
            dimension_semantics=("arbitrary",), vmem_limit_bytes=VMEM_LIMIT_BYTES),
        name="moe_dispatch",
    )(start, eidx, pos, h)


def _combine_kernel(start_ref, eidx_ref, pos_ref, eidx_nxt_ref, pos_nxt_ref, gate_ref, base_ref, x_ref,
                    mod_ref, g_ref, b_ref, ys_ref, o_ref, buf, sems):
    tm = base_ref.shape[0]
    i = pl.program_id(0)
    n = pl.num_programs(0)

    def gather(e_ref, p_ref, slot_buf):
        def issue(t, carry):
            for k in range(TOP_K):
                slot = start_ref[e_ref[k, t]] + p_ref[k, t]
                pltpu.make_async_copy(ys_ref.at[pl.ds(slot, 1), :], buf.at[slot_buf, k, pl.ds(t, 1), :],
                                      sems.at[slot_buf]).start()
            return carry
        lax.fori_loop(0, tm, issue, 0)

    cur = i % 2
    for slot_buf in range(2):
        @pl.when((i == 0) & (cur == slot_buf))
        def _(slot_buf=slot_buf):
            gather(eidx_ref, pos_ref, slot_buf)

        @pl.when((i + 1 < n) & (cur != slot_buf))
        def _(slot_buf=slot_buf):
            gather(eidx_nxt_ref, pos_nxt_ref, slot_buf)

    for slot_buf in range(2):
        @pl.when(cur == slot_buf)
        def _(slot_buf=slot_buf):
            for k in range(TOP_K):
                pltpu.make_async_copy(ys_ref.at[pl.ds(0, tm), :], buf.at[slot_buf, k], sems.at[slot_buf]).wait()
            acc = base_ref[...]
            for k in range(TOP_K):
                acc = acc + gate_ref[:, k:k + 1] * buf[slot_buf, k]
            o_ref[...] = _residual_ln(x_ref[...], acc, mod_ref[MOD_GATE2:MOD_GATE2 + 1, :], g_ref[...], b_ref[...])


def combine(base, ys, gate_cols, start, eidx, pos, x, mods, ln_g, ln_b, layout):
    tt, dm = base.shape
    nt = tt // MOE_TILE
    smem_tok = pl.BlockSpec((None, 8, MOE_TILE), lambda i, st: (i, 0, 0), memory_space=pltpu.SMEM)
    smem_nxt = pl.BlockSpec((None, 8, MOE_TILE), lambda i, st: (jnp.minimum(i + 1, nt - 1), 0, 0),
                            memory_space=pltpu.SMEM)
    row = pl.BlockSpec((MOE_TILE, dm), lambda i, st: (i, 0))
    vec = pl.BlockSpec((1, dm), lambda i, st: (0, 0))
    return pl.pallas_call(
        _combine_kernel,
        out_shape=jax.ShapeDtypeStruct((tt, dm), F32),
        grid_spec=pltpu.PrefetchScalarGridSpec(
            num_scalar_prefetch=1,
            grid=(nt,),
            in_specs=[smem_tok, smem_tok, smem_nxt, smem_nxt,
                      pl.BlockSpec((MOE_TILE, 8), lambda i, st: (i, 0)), row, row,
                      pl.BlockSpec((None, 6, dm), lambda i, st: (layout.cond_row(i), 0, 0)), vec, vec,
                      pl.BlockSpec(memory_space=pl.ANY)],
            out_specs=row,
            scratch_shapes=[pltpu.VMEM((2, TOP_K, MOE_TILE, dm), F32), pltpu.SemaphoreType.DMA((2,))]),
        compiler_params=pltpu.CompilerParams(
            dimension_semantics=("arbitrary",), vmem_limit_bytes=VMEM_LIMIT_BYTES),
        name="moe_combine",
    )(start, eidx, pos, eidx, pos, gate_cols, base, x, mods, ln_g[None, :], ln_b[None, :], ys)


def _ffn_body(x, wg_ref, wu_ref, wd_ref, row_lo, row_hi):
    x = x.astype(BF16)
    h1 = _dot(x, wg_ref[...].astype(BF16))
    h2 = _dot(x, wu_ref[...].astype(BF16))
    h = (h1 * jax.nn.sigmoid(h1)) * h2
    if row_lo is not None:
        row = lax.broadcasted_iota(jnp.int32, h.shape, 0)
        h = jnp.where((row >= row_lo) & (row < row_hi), h, 0.0)
    return _dot(h.astype(BF16), wd_ref[...].astype(BF16))


def _ffn_items_kernel(tile_ref, exp_ref, lo_ref, hi_ref, first_ref, n_ref, x_ref, wg_ref, wu_ref, wd_ref, o_ref):
    del tile_ref, exp_ref
    w = pl.program_id(0)

    @pl.when(w < n_ref[0])
    def _():
        y = _ffn_body(x_ref[...], wg_ref, wu_ref, wd_ref, lo_ref[w], hi_ref[w])

        @pl.when(first_ref[w] == 1)
        def _():
            o_ref[...] = y

        @pl.when(first_ref[w] == 0)
        def _():
            o_ref[...] += y


def expert_ffn(xs, items, wg, wu, wd, layer):
    rows, dm = xs.shape
    de = wg.shape[-1]
    n_items = items[0].shape[0]

    def row_map(w, tile, exp, lo, hi, first, n):
        return (tile[w], 0)

    def w_map(w, tile, exp, lo, hi, first, n):
        return (layer, exp[w], 0, 0)

    return pl.pallas_call(
        _ffn_items_kernel,
        out_shape=jax.ShapeDtypeStruct((rows, dm), F32),
        grid_spec=pltpu.PrefetchScalarGridSpec(
            num_scalar_prefetch=6,
            grid=(n_items,),
            in_specs=[pl.BlockSpec((MOE_TILE, dm), row_map),
                      pl.BlockSpec((None, None, dm, de), w_map),
                      pl.BlockSpec((None, None, dm, de), w_map),
                      pl.BlockSpec((None, None, de, dm), w_map)],
            out_specs=pl.BlockSpec((MOE_TILE, dm), row_map)),
        compiler_params=pltpu.CompilerParams(
            dimension_semantics=("arbitrary",), vmem_limit_bytes=VMEM_LIMIT_BYTES),
        name="expert_ffn",
    )(*items, xs, wg, wu, wd)


def _ffn_dense_kernel(x_ref, wg_ref, wu_ref, wd_ref, o_ref):
    o_ref[...] = _ffn_body(x_ref[...], wg_ref, wu_ref, wd_ref, None, None)


def shared_ffn(x, wg, wu, wd):
    tt, dm = x.shape
    de = wg.shape[-1]
    row = pl.BlockSpec((MOE_TILE, dm), lambda i: (i, 0))
    return pl.pallas_call(
        _ffn_dense_kernel,
        out_shape=jax.ShapeDtypeStruct((tt, dm), F32),
        grid=(tt // MOE_TILE,),
        in_specs=[row, pl.BlockSpec((dm, de), lambda i: (0, 0)), pl.BlockSpec((dm, de), lambda i: (0, 0)),
                  pl.BlockSpec((de, dm), lambda i: (0, 0))],
        out_specs=row,
        compiler_params=pltpu.CompilerParams(
            dimension_semantics=("arbitrary",), vmem_limit_bytes=VMEM_LIMIT_BYTES),
        name="shared_ffn",
    )(x, wg, wu, wd)


def layer_norm(x, g, b):
    mu = jnp.mean(x, axis=-1, keepdims=True)
    var = jnp.mean(jnp.square(x - mu), axis=-1, keepdims=True)
    return (x - mu) * lax.rsqrt(var + LN_EPS) * g + b


def _ffn_items(counts, n_rows):
    n_tiles = n_rows // MOE_TILE
    n_items = n_tiles + N_EXPERTS - 1
    end = jnp.cumsum(counts)
    start = end - counts
    first_tile = start // MOE_TILE
    tiles_of = jnp.where(counts > 0, (end - 1) // MOE_TILE - first_tile + 1, 0)
    item_end = jnp.cumsum(tiles_of)
    item_start = item_end - tiles_of
    total = item_end[-1]
    w = jnp.minimum(jnp.arange(n_items, dtype=jnp.int32), total - 1)
    exp = jnp.sum((item_end[None, :] <= w[:, None]).astype(jnp.int32), axis=1)
    onehot = (exp[:, None] == jnp.arange(N_EXPERTS, dtype=jnp.int32)[None, :]).astype(jnp.int32)
    pick = lambda v: jnp.sum(onehot * v[None, :], axis=1)
    tile = pick(first_tile) + (w - pick(item_start))
    lo = jnp.maximum(pick(start) - tile * MOE_TILE, 0)
    hi = jnp.minimum(pick(end) - tile * MOE_TILE, MOE_TILE)
    prev_tile = jnp.concatenate([jnp.full((1,), -1, jnp.int32), tile[:-1]])
    first = (tile != prev_tile).astype(jnp.int32)
    i32 = lambda v: v.astype(jnp.int32)
    return start.astype(jnp.int32), (i32(tile), i32(exp), i32(lo), i32(hi), first, i32(total).reshape(1))


def moe_sublayer(x, t, mods, ln_g, ln_b, layout, layer, w_router, router_bias, w_gate, w_up, w_down,
                 ws_gate, ws_up, ws_down):
    tt, dm = t.shape
    eidx, pos, gate, counts = route(t, w_router, router_bias)
    start, items = _ffn_items(counts, tt * TOP_K)
    xs = dispatch(t, start, eidx, pos)
    ys = expert_ffn(xs, items, w_gate, w_up, w_down, layer)
    shared = shared_ffn(t, ws_gate, ws_up, ws_down)
    gate_cols = jnp.swapaxes(gate, 1, 2).reshape(tt, 8)
    return combine(shared, ys, gate_cols, start, eidx, pos, x, mods, ln_g, ln_b, layout)


TOKEN_TILE = 256
MOD_SHIFT1, MOD_SCALE1, MOD_GATE1, MOD_SHIFT2, MOD_SCALE2, MOD_GATE2 = range(6)


class Layout:
    def __init__(self, n_ctx_seq, ctx_len, n_lat_seq, lat_len):
        self.n_ctx_seq, self.ctx_len, self.n_lat_seq, self.lat_len = n_ctx_seq, ctx_len, n_lat_seq, lat_len
        self.n_ctx = n_ctx_seq * ctx_len
        self.n_tok = self.n_ctx + n_lat_seq * lat_len
        assert ctx_len % TOKEN_TILE == 0 and lat_len % TOKEN_TILE == 0

    def cond_row(self, tile):
        ctx_tiles = self.n_ctx // TOKEN_TILE
        return jnp.where(tile < ctx_tiles, 0, 1 + (tile - ctx_tiles) // (self.lat_len // TOKEN_TILE))

    def tile_pos(self, tile):
        ctx_tiles = self.n_ctx // TOKEN_TILE
        per_ctx, per_lat = self.ctx_len // TOKEN_TILE, self.lat_len // TOKEN_TILE
        is_ctx = tile < ctx_tiles
        return (jnp.where(is_ctx, tile % per_ctx, (tile - ctx_tiles) % per_lat),
                jnp.where(is_ctx, per_ctx, per_lat))


def _premix_kernel(x_ref, prev_ref, next_ref, mod_ref, mp_ref, mn_ref, *out_refs, layout):
    tm = x_ref.shape[0]
    pos, per_seq = layout.tile_pos(pl.program_id(0))
    shift, scale = mod_ref[MOD_SHIFT1:MOD_SHIFT1 + 1, :], mod_ref[MOD_SCALE1:MOD_SCALE1 + 1, :]
    mod = lambda t: t * (1.0 + scale) + shift
    h = mod(x_ref[...])
    h_before = jnp.where(pos > 0, mod(prev_ref[...]), 0.0)
    h_after = jnp.where(pos < per_seq - 1, mod(next_ref[...]), 0.0)
    row = lax.broadcasted_iota(jnp.int32, h.shape, 0)
    d_prev = jnp.where(row == 0, h_before, pltpu.roll(h, 1, 0)) - h
    d_next = jnp.where(row == tm - 1, h_after, pltpu.roll(h, tm - 1, 0)) - h
    for i, o_ref in enumerate(out_refs):
        o_ref[...] = (h + mp_ref[i:i + 1, :] * d_prev + mn_ref[i:i + 1, :] * d_next).astype(BF16)


def premix(x, mods, mix_prev, mix_next, layout):
    tt, dm = x.shape
    nt = tt // TOKEN_TILE
    zero = jnp.zeros((1, dm), F32)
    prev_rows = jnp.concatenate([zero, x[TOKEN_TILE - 1::TOKEN_TILE][:-1]], axis=0)[:, None, :]
    next_rows = jnp.concatenate([x[::TOKEN_TILE][1:], zero], axis=0)[:, None, :]
    tile = pl.BlockSpec((TOKEN_TILE, dm), lambda i: (i, 0))
    edge = pl.BlockSpec((None, 1, dm), lambda i: (i, 0, 0))
    par = pl.BlockSpec((6, dm), lambda i: (0, 0))
    return pl.pallas_call(
        functools.partial(_premix_kernel, layout=layout),
        out_shape=[jax.ShapeDtypeStruct((tt, dm), BF16)] * 6,
        grid=(nt,),
        in_specs=[tile, edge, edge, pl.BlockSpec((None, 6, dm), lambda i: (layout.cond_row(i), 0, 0)), par, par],
        out_specs=[tile] * 6,
        compiler_params=pltpu.CompilerParams(
            dimension_semantics=("arbitrary",), vmem_limit_bytes=VMEM_LIMIT_BYTES),
        name="premix",
    )(x, prev_rows, next_rows, mods, mix_prev, mix_next)


def _residual_ln(x, y, gate, g, b):
    z = ALPHA * x + gate * y
    mu = jnp.mean(z, axis=-1, keepdims=True)
    zc = z - mu
    var = jnp.mean(zc * zc, axis=-1, keepdims=True)
    return zc * lax.rsqrt(var + LN_EPS) * g + b


def _post_kernel(x_ref, y_ref, mod_ref, g_ref, b_ref, *out_refs, gate_row, mod_rows):
    xn = _residual_ln(x_ref[...], y_ref[...], mod_ref[gate_row:gate_row + 1, :], g_ref[...], b_ref[...])
    out_refs[0][...] = xn
    if mod_rows is not None:
        sh, sc = mod_rows
        out_refs[1][...] = xn * (1.0 + mod_ref[sc:sc + 1, :]) + mod_ref[sh:sh + 1, :]


def post(x, y, mods, ln_g, ln_b, layout, gate_row, mod_rows):
    tt, dm = x.shape
    tile = pl.BlockSpec((TOKEN_TILE, dm), lambda i: (i, 0))
    vec = pl.BlockSpec((1, dm), lambda i: (0, 0))
    n_out = 1 if mod_rows is None else 2
    return pl.pallas_call(
        functools.partial(_post_kernel, gate_row=gate_row, mod_rows=mod_rows),
        out_shape=[jax.ShapeDtypeStruct((tt, dm), F32)] * n_out,
        grid=(tt // TOKEN_TILE,),
        in_specs=[tile, tile, pl.BlockSpec((None, 6, dm), lambda i: (layout.cond_row(i), 0, 0)), vec, vec],
        out_specs=[tile] * n_out,
        compiler_params=pltpu.CompilerParams(
            dimension_semantics=("arbitrary",), vmem_limit_bytes=VMEM_LIMIT_BYTES),
        name="post",
    )(x, y, mods, ln_g[None, :], ln_b[None, :])


def _pool_kernel(x_ref, mod_ref, w_ref, scale_ref, o_ref, *, grid_w):
    seq_len = x_ref.shape[0]
    t_idx = lax.broadcasted_iota(jnp.int32, x_ref.shape, 0)

    def shifted(v, k, pos, extent, stride):
        if k > 0:
            return jnp.where(pos < extent - k, pltpu.roll(v, seq_len - k * stride, 0), 0.0)
        return jnp.where(pos >= -k, pltpu.roll(v, -k * stride, 0), 0.0)

    def box(v, w, pos, extent, stride):
        m = w // 2
        ahead, behind, step = v, v, 1
        while step < m:
            ahead = ahead + shifted(ahead, step, pos, extent, stride)
            behind = behind + shifted(behind, -step, pos, extent, stride)
            step *= 2
        total = ahead + shifted(behind, -1, pos, extent, stride)
        cnt = jnp.minimum(pos + m, extent) - jnp.maximum(pos - m, 0)
        return total, cnt.astype(F32)

    h = x_ref[...] * (1.0 + mod_ref[MOD_SCALE1:MOD_SCALE1 + 1, :]) + mod_ref[MOD_SHIFT1:MOD_SHIFT1 + 1, :]
    for gi, w in enumerate(POOL_WINDOWS):
        @pl.when(pl.program_id(1) == gi)
        def _(w=w):
            if grid_w is None:
                s, cnt = box(h, w, t_idx, seq_len, 1)
            else:
                s, cr = box(h, w, t_idx // grid_w, seq_len // grid_w, grid_w)
                s, cc = box(s, w, t_idx % grid_w, grid_w, 1)
                cnt = cr * cc
            p = (s / cnt - h).astype(BF16)
            o_ref[...] = _dot(p, w_ref[...].astype(BF16)) * scale_ref[...]


def pool_mix(x, mods, pool_w, pool_scale, layout, latent):
    dm = x.shape[1]
    n_grp = len(POOL_WINDOWS)
    if latent:
        n_seq, seq_len, first_blk, cond0 = layout.n_lat_seq, layout.lat_len, layout.n_ctx // layout.lat_len, 1
    else:
        n_seq, seq_len, first_blk, cond0 = layout.n_ctx_seq, layout.ctx_len, 0, None
    cond = (lambda i: 0) if cond0 is None else (lambda i: cond0 + i)
    return pl.pallas_call(
        functools.partial(_pool_kernel, grid_w=GRID_W if latent else None),
        out_shape=jax.ShapeDtypeStruct((n_seq * seq_len, dm), F32),
        grid=(n_seq, n_grp),
        in_specs=[pl.BlockSpec((seq_len, POOL_GROUP), lambda i, g: (first_blk + i, g)),
                  pl.BlockSpec((None, 6, POOL_GROUP), lambda i, g: (cond(i), 0, g)),
                  pl.BlockSpec((None, POOL_GROUP, POOL_GROUP), lambda i, g: (g, 0, 0)),
                  pl.BlockSpec((1, POOL_GROUP), lambda i, g: (0, g))],
        out_specs=pl.BlockSpec((seq_len, POOL_GROUP), lambda i, g: (i, g)),
        compiler_params=pltpu.CompilerParams(
            dimension_semantics=("arbitrary", "arbitrary"), vmem_limit_bytes=VMEM_LIMIT_BYTES),
        name="pool_mix",
    )(x, mods, pool_w, pool_scale[None, :])


def kernel(x_prompt, x_sample, c, state_rwkv, c_ctx, w_ada, b_ada, ln_g, ln_b, rw_mix_prev, rw_mix_next, rw_w_r, rw_w_k, rw_w_v, rw_w_o, rw_w0, rw_w1, rw_w2, rw_a0, rw_a1, rw_a2, rw_g1, rw_g2, rw_k_k, rw_k_a, rw_r_k, rw_gn_g, rw_gn_b, pool_w, pool_scale, moe_router, moe_router_bias, moe_w_gate, moe_w_up, moe_w_down, moe_ws_gate, moe_ws_up, moe_ws_down):
    dm = D_MODEL
    H, N = RWKV_HEADS, RWKV_HEAD
    bc, lc_, _ = x_prompt.shape
    bl, ll, _ = x_sample.shape
    layout = Layout(bc, lc_, bl, ll)
    tc = layout.n_ctx
    x = jnp.concatenate([x_prompt.reshape(tc, dm), x_sample.reshape(bl * ll, dm)], axis=0)
    cond = jnp.concatenate([c_ctx[None, :], c], axis=0)
    n_cond = cond.shape[0]
    assert n_cond <= 8
    cond_pad = jnp.zeros((8, dm), F32).at[:n_cond].set(jax.nn.silu(cond))
    new_states = []

    for l in range(DEPTH):
        mods = (matmul(cond_pad, w_ada, tn=1024, layer=l) + b_ada[l]).reshape(8, 6, dm)
        j = l // 2
        if l % 2 == 0:
            xr, xw, xk, xv, xa, xg = premix(x, mods, rw_mix_prev[j], rw_mix_next[j], layout)
            r = matmul(xr, rw_w_r[j])
            k = matmul(xk, rw_w_k[j])
            v = matmul(xv, rw_w_v[j])
            g = matmul(jax.nn.sigmoid(matmul(xg, rw_g1[j])), rw_g2[j])
            wl = [matmul(jnp.tanh(matmul(xw, rw_w1[j, d])), rw_w2[j, d]) for d in range(2)]
            al = [matmul(matmul(xa, rw_a1[j, d]), rw_a2[j, d]) for d in range(2)]
            par = jnp.zeros((16, dm), F32).at[:9].set(jnp.stack(
                [rw_w0[j, 0], rw_w0[j, 1], rw_a0[j, 0], rw_a0[j, 1], rw_k_k[j], rw_k_a[j],
                 rw_r_k[j].reshape(dm), rw_gn_g[j], rw_gn_b[j]]))
            out_c, s_fin = rwkv_core(r, k, v, g, wl, al, par, jnp.zeros((bc, 2, H, N, N), F32), 0, lc_)
            out_l, _ = rwkv_core(r, k, v, g, wl, al, par, state_rwkv[:, j], tc, ll)
            new_states.append(s_fin)
            mix = matmul(jnp.concatenate([out_c, out_l], axis=0), rw_w_o[j])
        else:
            mix = jnp.concatenate([pool_mix(x, mods, pool_w[j], pool_scale[j], layout, False),
                                   pool_mix(x, mods, pool_w[j], pool_scale[j], layout, True)], axis=0)
        x, h2 = post(x, mix, mods, ln_g[l, 0], ln_b[l, 0], layout, MOD_GATE1, (MOD_SHIFT2, MOD_SCALE2))
        x = moe_sublayer(x, h2, mods, ln_g[l, 1], ln_b[l, 1], layout, l, moe_router[l], moe_router_bias[l],
                         moe_w_gate, moe_w_up, moe_w_down, moe_ws_gate[l], moe_ws_up[l], moe_ws_down[l])

    return (x[:tc].reshape(bc, lc_, dm), x[tc:].reshape(bl, ll, dm), jnp.stack(new_states, axis=1))
```

```python
import functools

import jax
import jax.numpy as jnp
from jax import lax
from jax.experimental import pallas as pl
from jax.experimental.pallas import tpu as pltpu

D_MODEL = 2048
DEPTH = 2
GRID_W = 64
RWKV_HEAD = 64
RWKV_HEADS = D_MODEL // RWKV_HEAD
POOL_WINDOWS = (2, 4, 8, 16)
POOL_GROUP = D_MODEL // len(POOL_WINDOWS)
N_EXPERTS = 64
TOP_K = 6
N_GROUPS = 8
TOPK_GROUPS = 4
ROUTED_SCALE = 2.5
ALPHA = (2 * DEPTH) ** 0.25
LN_EPS = 1e-5
GN_EPS = 64e-5

F32 = jnp.float32
BF16 = jnp.bfloat16

VMEM_LIMIT_BYTES = 48 * 1024 * 1024
FFN_VMEM_LIMIT_BYTES = 56 * 1024 * 1024

SCAN_CHUNK = 64
SCAN_HEADS = 4
SCAN_LANES = SCAN_HEADS * RWKV_HEAD
SCAN_PREP_CHUNKS = 4
MOE_TILE = 256


def _dot(a, b):
    return jnp.dot(a, b, preferred_element_type=F32)


def _dot_nt(a, b):
    return lax.dot_general(a, b, (((1,), (1,)), ((), ())), preferred_element_type=F32)


def _dot_tn(a, b):
    return lax.dot_general(a, b, (((0,), (0,)), ((), ())), preferred_element_type=F32)


def _mm_kernel(x_ref, w_ref, o_ref):
    o_ref[...] = _dot(x_ref[...].astype(BF16), w_ref[...].astype(BF16))


def matmul(x, w, *, tm=1024, tn=512, layer=None):
    m, k = x.shape
    n = w.shape[-1]
    tm = min(tm, m)
    tn = min(tn, n)
    assert m % tm == 0 and n % tn == 0
    if layer is None:
        w_spec = pl.BlockSpec((k, tn), lambda j, i: (0, j))
    else:
        w_spec = pl.BlockSpec((None, k, tn), lambda j, i: (layer, 0, j))
    return pl.pallas_call(
        _mm_kernel,
        out_shape=jax.ShapeDtypeStruct((m, n), F32),
        grid=(n // tn, m // tm),
        in_specs=[pl.BlockSpec((tm, k), lambda j, i: (i, 0)), w_spec],
        out_specs=pl.BlockSpec((tm, tn), lambda j, i: (i, j)),
        compiler_params=pltpu.CompilerParams(
            dimension_semantics=("arbitrary", "arbitrary"),
            vmem_limit_bytes=VMEM_LIMIT_BYTES),
        name="matmul",
    )(x, w)


def _split3(x):
    hi = x.astype(BF16)
    r1 = x - hi.astype(F32)
    mid = r1.astype(BF16)
    lo = (r1 - mid.astype(F32)).astype(BF16)
    return hi, mid, lo


def _softplus(z):
    return jnp.maximum(z, 0.0) + jnp.log(1.0 + jnp.exp(-jnp.abs(z)))


def _scan_kernel(r_ref, k_ref, v_ref, g_ref, wl0_ref, wl1_ref, al0_ref, al1_ref, par_ref, s0_ref,
                 out_ref, sfin_ref, s_scr, kk_ref, lw_ref, a_ref, kd_ref, bonus_ref, y_ref,
                 pr_s, ak_s, lra_s, uy_s, gend_s, lvl_s, *, nc, prep):
    C, N, HB, W = SCAN_CHUNK, RWKV_HEAD, SCAN_HEADS, SCAN_LANES
    t_idx = lax.broadcasted_iota(jnp.int32, (C, W), 0)
    s_idx = lax.broadcasted_iota(jnp.int32, (C, W), 1) % C
    eye_row = (s_idx == t_idx).astype(F32)
    tri_t = lax.broadcasted_iota(jnp.int32, (C, C), 0)
    tri_s = lax.broadcasted_iota(jnp.int32, (C, C), 1)
    blk_r = lax.broadcasted_iota(jnp.int32, (W, W), 0) // N
    blk_c = lax.broadcasted_iota(jnp.int32, (W, W), 1) // N
    bd_mask = blk_r == blk_c
    bd_mask_bf = bd_mask.astype(BF16)

    def expand(x_row_bf):
        return jnp.concatenate([x_row_bf] * HB, axis=0) * bd_mask_bf

    for d in range(2):
        s_scr[d] = jnp.zeros((W, W), F32)
        for h in range(HB):
            s_scr[d, h * N:(h + 1) * N, h * N:(h + 1) * N] = s0_ref[d, h]
    y_ref[...] = jnp.zeros_like(y_ref)

    levels = []
    b = 2
    while b < C:
        levels.append(b)
        b *= 2

    @pl.when((pl.program_id(0) == 0) & (pl.program_id(1) == 0))
    def _():
        t_bd = lax.broadcasted_iota(jnp.int32, (W, W), 0) % C
        s_bd = lax.broadcasted_iota(jnp.int32, (W, W), 1) % C
        for li, b in enumerate(levels):
            tb, sb = t_bd // b, s_bd // b
            lvl_s[0, li] = (bd_mask & (tb % 2 == 1) & (sb == tb - 1)).astype(BF16)
            lvl_s[1, li] = (bd_mask & (tb % 2 == 0) & (sb == tb + 1)).astype(BF16)

    ones_bd = bd_mask_bf
    RB = 4 * C
    par = lambda i: par_ref[i:i + 1, :]

    def head_sum(x):
        hi, lo = _split2(x)
        return _dot(hi, ones_bd) + _dot(lo, ones_bd)

    def prologue(i, carry):
        rows = pl.ds(pl.multiple_of(i * RB, RB), RB)
        k = k_ref[rows, :]
        kkp = k * par(4)
        kk_ref[rows, :] = kkp * lax.rsqrt(jnp.maximum(head_sum(kkp * kkp), 1e-24))
        kd_sum = jnp.zeros_like(k)
        for d, (wl_ref, al_ref) in enumerate(((wl0_ref, al0_ref), (wl1_ref, al1_ref))):
            w_log = -_softplus(-(par(d) + wl_ref[rows, :])) - 0.5
            lw_ref[d, rows, :] = -jnp.exp(w_log)
            a = jax.nn.sigmoid(par(2 + d) + al_ref[rows, :])
            a_ref[d, rows, :] = a
            kd = k * (1.0 + (a - 1.0) * par(5))
            kd_ref[d, rows, :] = kd
            kd_sum = kd_sum + kd
        bonus_ref[rows, :] = head_sum(r_ref[rows, :] * kd_sum * par(6)) * v_ref[rows, :]
        return carry

    lax.fori_loop(0, nc * C // RB, prologue, 0)

    def interleave(chains):
        live = list(chains)
        while live:
            nxt = []
            for ch in live:
                try:
                    next(ch)
                    nxt.append(ch)
                except StopIteration:
                    pass
            live = nxt

    def prepare_chunk(d, ci):
        fwd = d == 0
        strict = (s_idx < t_idx) if fwd else (s_idx > t_idx)
        incl = (s_idx <= t_idx) if fwd else (s_idx >= t_idx)
        tri = ((tri_s <= tri_t) if fwd else (tri_s >= tri_t)).astype(BF16)
        rows = pl.ds(pl.multiple_of(ci * C, C), C)
        lw = lw_ref[d, rows, :]
        hi, mid, lo = _split3(lw)
        lc = _dot(tri, hi) + _dot(tri, mid) + _dot(tri, lo)
        yield
        kk = kk_ref[rows, :]
        g_in = jnp.exp(lc)
        g_inv = jnp.exp(-lc)
        g_ex = jnp.exp(lc - lw)
        last = C - 1 if fwd else 0
        gend_s[d, ci] = jnp.broadcast_to(g_in[last:last + 1, :], (8, W))
        bt = (kk * g_ex).astype(BF16)
        rt = (r_ref[rows, :] * g_in).astype(BF16)
        at = (-(kk * a_ref[d, rows, :]) * g_inv).astype(BF16)
        kt = (kd_ref[d, rows, :] * g_inv).astype(BF16)
        br = jnp.concatenate([bt, rt], axis=0)
        ak_s[d, ci] = jnp.concatenate([at, kt], axis=0)
        la = _dot_nt(br, expand(at))
        lk = _dot_nt(br, expand(kt))
        yield
        lba = la[:C]
        lra_s[d, ci] = jnp.where(incl, la[C:], 0.0).astype(BF16)
        lbrk = jnp.concatenate([jnp.where(strict, lk[:C], 0.0), jnp.where(incl, lk[C:], 0.0)], axis=0)
        wy2 = _dot(lbrk.astype(BF16), expand(v_ref[rows, :].astype(BF16)))

        pair = (s_idx == t_idx - 1) if fwd else (s_idx == t_idx + 1)
        inv = eye_row + jnp.where(pair & (t_idx % 2 == (1 if fwd else 0)), lba, 0.0)
        lba4 = jnp.concatenate([lba.astype(BF16)] * HB, axis=0)
        for li in range(len(levels)):
            m1 = _dot(inv.astype(BF16), lba4 * lvl_s[d, li])
            yield
            inv = inv + _dot(m1.astype(BF16), expand(inv.astype(BF16)))
            yield
        inv_bf = inv.astype(BF16)
        pu = _dot(inv_bf, expand(bt))
        u0 = _dot(inv_bf, expand(wy2[:C].astype(BF16)))
        yield
        pr_s[d, ci] = jnp.concatenate([pu.astype(BF16), rt], axis=0)
        uy_s[d, ci] = jnp.concatenate([u0, wy2[C:]], axis=0)

    def prepare(i, carry):
        interleave([prepare_chunk(d, i * prep + j) for j in range(prep) for d in range(2)])
        return carry

    lax.fori_loop(0, nc // prep, prepare, 0)

    def advance_chunk(d, ci):
        rows = pl.ds(pl.multiple_of(ci * C, C), C)
        s_bd = s_scr[d]
        py = _dot_nt(pr_s[d, ci], s_bd.astype(BF16))
        yield
        uy = uy_s[d, ci]
        u_bf = (py[:C] + uy[:C]).astype(BF16)
        upd = _dot_tn(jnp.concatenate([u_bf, v_ref[rows, :].astype(BF16)], axis=0), ak_s[d, ci])
        y = py[C:] + uy[C:] + _dot(lra_s[d, ci], expand(u_bf))
        yield
        y_ref[rows, :] += y
        s_scr[d] = jnp.where(bd_mask, (s_bd + upd) * gend_s[d, ci][0:1, :], 0.0)

    def advance(c, carry):
        interleave([advance_chunk(0, c), advance_chunk(1, nc - 1 - c)])
        return carry

    lax.fori_loop(0, nc, advance, 0)

    for d in range(2):
        for h in range(HB):
            sfin_ref[d, h] = s_scr[d, h * N:(h + 1) * N, h * N:(h + 1) * N]

    def epilogue(i, carry):
        rows = pl.ds(pl.multiple_of(i * RB, RB), RB)
        y = y_ref[rows, :]
        yc = y - head_sum(y) * (1.0 / N)
        var = head_sum(yc * yc) * (1.0 / N)
        yn = yc * lax.rsqrt(var + GN_EPS) * par(7) + par(8)
        out_ref[rows, :] = ((yn + bonus_ref[rows, :]) * g_ref[rows, :]).astype(BF16)
        return carry

    lax.fori_loop(0, nc * C // RB, epilogue, 0)


def rwkv_core(r, k, v, g, wl, al, par, s0, first_row, seq_len):
    b = s0.shape[0]
    d = r.shape[1]
    l = seq_len
    assert first_row % l == 0
    nc = l // SCAN_CHUNK
    first_blk = first_row // l
    seq = pl.BlockSpec((l, SCAN_LANES), lambda i, j: (first_blk + i, j))
    oseq = pl.BlockSpec((l, SCAN_LANES), lambda i, j: (i, j))
    st = pl.BlockSpec((None, 2, SCAN_HEADS, RWKV_HEAD, RWKV_HEAD), lambda i, j: (i, 0, j, 0, 0))
    seq_scr = pltpu.VMEM((l, SCAN_LANES), F32)
    seq2_scr = pltpu.VMEM((2, l, SCAN_LANES), F32)
    C, W = SCAN_CHUNK, SCAN_LANES
    per_chunk = lambda rows, dt: pltpu.VMEM((2, nc, rows, W), dt)
    return pl.pallas_call(
        functools.partial(_scan_kernel, nc=nc, prep=SCAN_PREP_CHUNKS),
        out_shape=(jax.ShapeDtypeStruct((b * l, d), BF16), jax.ShapeDtypeStruct(s0.shape, F32)),
        grid=(b, d // SCAN_LANES),
        in_specs=[seq] * 8 + [pl.BlockSpec((16, SCAN_LANES), lambda i, j: (0, j)), st],
        out_specs=(oseq, st),
        scratch_shapes=[pltpu.VMEM((2, SCAN_LANES, SCAN_LANES), F32),
                        seq_scr, seq2_scr, seq2_scr, seq2_scr, seq_scr, seq_scr,
                        per_chunk(2 * C, BF16), per_chunk(2 * C, BF16), per_chunk(C, BF16),
                        per_chunk(2 * C, F32), per_chunk(8, F32),
                        pltpu.VMEM((2, 5, W, W), BF16)],
        compiler_params=pltpu.CompilerParams(
            dimension_semantics=("arbitrary", "arbitrary"),
            vmem_limit_bytes=VMEM_LIMIT_BYTES),
        name="wkv_scan",
    )(r, k, v, g, wl[0], wl[1], al[0], al[1], par, s0)


def _split2(x):
    hi = x.astype(BF16)
    return hi, (x - hi.astype(F32)).astype(BF16)


def _first_index(hit_value, iota, sentinel, axis):
    return jnp.min(jnp.where(hit_value, iota, sentinel), axis=axis, keepdims=True)


def _router_kernel(h_ref, wrt_ref, bias_ref, eidx_ref, pos_ref, gate_ref, cnt_ref, cnt_scr):
    tm = h_ref.shape[0]
    E, G, GS = N_EXPERTS, N_GROUPS, N_EXPERTS // N_GROUPS
    neg = -jnp.inf

    @pl.when(pl.program_id(0) == 0)
    def _():
        cnt_scr[...] = jnp.zeros_like(cnt_scr)

    x_hi, x_lo = _split2(h_ref[...])
    w_hi, w_lo = _split2(wrt_ref[...])
    logits = _dot_nt(w_hi, x_hi) + (_dot_nt(w_hi, x_lo) + _dot_nt(w_lo, x_hi))
    scores = jax.nn.sigmoid(logits)
    biased = scores + bias_ref[...]

    g3 = biased.reshape(G, GS, tm)
    w_iota = lax.broadcasted_iota(jnp.int32, (G, GS, tm), 1).astype(F32)
    m1 = jnp.max(g3, axis=1, keepdims=True)
    first = _first_index(g3 == m1, w_iota, float(GS), 1)
    m2 = jnp.max(jnp.where(w_iota == first, neg, g3), axis=1, keepdims=True)
    gs = (m1 + m2).reshape(G, tm)

    g_iota = lax.broadcasted_iota(jnp.int32, (G, tm), 0).astype(F32)
    gsel = jnp.zeros((G, tm), jnp.bool_)
    cur = gs
    for _ in range(TOPK_GROUPS):
        m = jnp.max(cur, axis=0, keepdims=True)
        hit = g_iota == _first_index(cur == m, g_iota, float(G), 0)
        gsel = gsel | hit
        cur = jnp.where(hit, neg, cur)

    masked = jnp.where(gsel[:, None, :], g3, neg).reshape(E, tm)
    e_iota = lax.broadcasted_iota(jnp.int32, (E, tm), 0).astype(F32)
    sel = jnp.zeros((E, tm), jnp.bool_)
    hits, ids = [], []
    cur = masked
    for _ in range(TOP_K):
        m = jnp.max(cur, axis=0, keepdims=True)
        f = _first_index(cur == m, e_iota, float(E), 0)
        hit = e_iota == f
        hits.append(hit)
        ids.append(f)
        sel = sel | hit
        cur = jnp.where(hit, neg, cur)

    ssum = jnp.sum(jnp.where(sel, scores, 0.0), axis=0, keepdims=True)
    gates = scores / (ssum + 1e-20) * ROUTED_SCALE

    before = (lax.broadcasted_iota(jnp.int32, (tm, tm), 0) < lax.broadcasted_iota(jnp.int32, (tm, tm), 1))
    sel_f = sel.astype(F32)
    rank = _dot(sel_f.astype(BF16), before.astype(BF16)) + cnt_scr[:, 0:1]
    cnt_new = cnt_scr[...] + jnp.sum(sel_f, axis=1, keepdims=True)
    cnt_scr[...] = cnt_new
    cnt_ref[...] = cnt_new.astype(jnp.int32)

    k_iota = lax.broadcasted_iota(jnp.int32, (8, tm), 0)
    eidx8 = jnp.zeros((8, tm), F32)
    pos8 = jnp.zeros((8, tm), F32)
    gate8 = jnp.zeros((8, tm), F32)
    for k in range(TOP_K):
        row = k_iota == k
        eidx8 = jnp.where(row, ids[k], eidx8)
        pos8 = jnp.where(row, jnp.sum(jnp.where(hits[k], rank, 0.0), axis=0, keepdims=True), pos8)
        gate8 = jnp.where(row, jnp.sum(jnp.where(hits[k], gates, 0.0), axis=0, keepdims=True), gate8)
    eidx_ref[...] = eidx8.astype(jnp.int32)
    pos_ref[...] = pos8.astype(jnp.int32)
    gate_ref[...] = gate8


def route(h, w_router, router_bias):
    tt, dm = h.shape
    nt = tt // MOE_TILE
    tok = pl.BlockSpec((None, 8, MOE_TILE), lambda i: (i, 0, 0))
    eidx, pos, gate, cnt = pl.pallas_call(
        _router_kernel,
        out_shape=(jax.ShapeDtypeStruct((nt, 8, MOE_TILE), jnp.int32),
                   jax.ShapeDtypeStruct((nt, 8, MOE_TILE), jnp.int32),
                   jax.ShapeDtypeStruct((nt, 8, MOE_TILE), F32),
                   jax.ShapeDtypeStruct((N_EXPERTS, 128), jnp.int32)),
        grid=(nt,),
        in_specs=[pl.BlockSpec((MOE_TILE, dm), lambda i: (i, 0)),
                  pl.BlockSpec((N_EXPERTS, dm), lambda i: (0, 0)),
                  pl.BlockSpec((N_EXPERTS, 1), lambda i: (0, 0))],
        out_specs=(tok, tok, tok, pl.BlockSpec((N_EXPERTS, 128), lambda i: (0, 0))),
        scratch_shapes=[pltpu.VMEM((N_EXPERTS, 128), F32)],
        compiler_params=pltpu.CompilerParams(
            dimension_semantics=("arbitrary",), vmem_limit_bytes=VMEM_LIMIT_BYTES),
        name="moe_router",
    )(h, w_router.T, router_bias[:, None])
    return eidx, pos, gate, cnt[:, 0]


def _dispatch_kernel(slot_ref, x_ref, xs_ref, sem):
    tm = x_ref.shape[0]

    def issue(t, carry):
        for k in range(TOP_K):
            pltpu.make_async_copy(x_ref.at[pl.ds(t, 1), :], xs_ref.at[pl.ds(slot_ref[k, t], 1), :], sem).start()
        return carry

    lax.fori_loop(0, tm, issue, 0)
    for k in range(TOP_K):
        pltpu.make_async_copy(x_ref, xs_ref.at[pl.ds(0, tm), :], sem).wait()


def dispatch(h, slot):
    tt, dm = h.shape
    nt = tt // MOE_TILE
    return pl.pallas_call(
        _dispatch_kernel,
        out_shape=jax.ShapeDtypeStruct((tt * TOP_K, dm), F32),
        grid=(nt,),
        in_specs=[pl.BlockSpec((None, 8, MOE_TILE), lambda i: (i, 0, 0), memory_space=pltpu.SMEM),
                  pl.BlockSpec((MOE_TILE, dm), lambda i: (i, 0))],
        out_specs=pl.BlockSpec(memory_space=pl.ANY),
        scratch_shapes=[pltpu.SemaphoreType.DMA],
        compiler_params=pltpu.CompilerParams(
            dimension_semantics=("arbitrary",), vmem_limit_bytes=VMEM_LIMIT_BYTES),
        name="moe_dispatch",
    )(slot, h)


def _combine_kernel(slot_ref, slot_nxt_ref, gate_ref, base_ref, x_ref,
                    mod_ref, g_ref, b_ref, ys_ref, o_ref, buf, sems):
    tm = base_ref.shape[0]
    i = pl.program_id(0)
    n = pl.num_programs(0)

    def gather(s_ref, slot_buf):
        def issue(t, carry):
            for k in range(TOP_K):
                pltpu.make_async_copy(ys_ref.at[pl.ds(s_ref[k, t], 1), :], buf.at[slot_buf, k, pl.ds(t, 1), :],
                                      sems.at[slot_buf]).start()
            return carry
        lax.fori_loop(0, tm, issue, 0)

    cur = i % 2
    for slot_buf in range(2):
        @pl.when((i == 0) & (cur == slot_buf))
        def _(slot_buf=slot_buf):
            gather(slot_ref, slot_buf)

        @pl.when((i + 1 < n) & (cur != slot_buf))
        def _(slot_buf=slot_buf):
            gather(slot_nxt_ref, slot_buf)

    for slot_buf in range(2):
        @pl.when(cur == slot_buf)
        def _(slot_buf=slot_buf):
            for k in range(TOP_K):
                pltpu.make_async_copy(ys_ref.at[pl.ds(0, tm), :], buf.at[slot_buf, k], sems.at[slot_buf]).wait()
            acc = base_ref[...]
            for k in range(TOP_K):
                acc = acc + gate_ref[:, k:k + 1] * buf[slot_buf, k]
            o_ref[...] = _residual_ln(x_ref[...], acc, mod_ref[MOD_GATE2:MOD_GATE2 + 1, :], g_ref[...], b_ref[...])


def combine(base, ys, gate_cols, slot, x, mods, ln_g, ln_b, layout):
    tt, dm = base.shape
    nt = tt // MOE_TILE
    smem_tok = pl.BlockSpec((None, 8, MOE_TILE), lambda i: (i, 0, 0), memory_space=pltpu.SMEM)
    smem_nxt = pl.BlockSpec((None, 8, MOE_TILE), lambda i: (jnp.minimum(i + 1, nt - 1), 0, 0),
                            memory_space=pltpu.SMEM)
    row = pl.BlockSpec((MOE_TILE, dm), lambda i: (i, 0))
    vec = pl.BlockSpec((1, dm), lambda i: (0, 0))
    return pl.pallas_call(
        _combine_kernel,
        out_shape=jax.ShapeDtypeStruct((tt, dm), F32),
        grid=(nt,),
        in_specs=[smem_tok, smem_nxt, pl.BlockSpec((MOE_TILE, 8), lambda i: (i, 0)), row, row,
                  pl.BlockSpec((None, 6, dm), lambda i: (layout.cond_row(i), 0, 0)), vec, vec,
                  pl.BlockSpec(memory_space=pl.ANY)],
        out_specs=row,
        scratch_shapes=[pltpu.VMEM((2, TOP_K, MOE_TILE, dm), F32), pltpu.SemaphoreType.DMA((2,))],
        compiler_params=pltpu.CompilerParams(
            dimension_semantics=("arbitrary",), vmem_limit_bytes=VMEM_LIMIT_BYTES),
        name="moe_combine",
    )(slot, slot, gate_cols, base, x, mods, ln_g[None, :], ln_b[None, :], ys)


def _ffn_body(x, wg_ref, wu_ref, wd_ref, row_lo, row_hi):
    x = x.astype(BF16)
    h1 = _dot(x, wg_ref[...].astype(BF16))
    h2 = _dot(x, wu_ref[...].astype(BF16))
    h = (h1 * jax.nn.sigmoid(h1)) * h2
    if row_lo is not None:
        row = lax.broadcasted_iota(jnp.int32, h.shape, 0)
        h = jnp.where((row >= row_lo) & (row < row_hi), h, 0.0)
    return _dot(h.astype(BF16), wd_ref[...].astype(BF16))


def _ffn_items_kernel(tile_ref, exp_ref, lo_ref, hi_ref, first_ref, n_ref, newexp_ref, nxt_ref, par_ref,
                      x_ref, wg_hbm, wu_hbm, wd_hbm, o_ref, wg_buf, wu_buf, wd_buf, wg_bf, wu_bf, wd_bf, sems,
                      *, layer):
    del tile_ref
    w = pl.program_id(0)

    def weight_copies(e, buf):
        return (pltpu.make_async_copy(wg_hbm.at[layer, e], wg_buf.at[buf], sems.at[buf]),
                pltpu.make_async_copy(wu_hbm.at[layer, e], wu_buf.at[buf], sems.at[buf]),
                pltpu.make_async_copy(wd_hbm.at[layer, e], wd_buf.at[buf], sems.at[buf]))

    @pl.when(w == 0)
    def _():
        for cp in weight_copies(exp_ref[0], 0):
            cp.start()

    @pl.when((w < n_ref[0]) & (newexp_ref[w] == 1))
    def _():
        for buf in range(2):
            @pl.when(par_ref[w] == buf)
            def _(buf=buf):
                for cp in weight_copies(exp_ref[w], buf):
                    cp.wait()

                @pl.when(nxt_ref[w] >= 0)
                def _():
                    for cp in weight_copies(nxt_ref[w], 1 - buf):
                        cp.start()

                wg_bf[...] = wg_buf[buf].astype(BF16)
                wu_bf[...] = wu_buf[buf].astype(BF16)
                wd_bf[...] = wd_buf[buf].astype(BF16)

    @pl.when(w < n_ref[0])
    def _():
        y = _ffn_body(x_ref[...], wg_bf, wu_bf, wd_bf, lo_ref[w], hi_ref[w])

        @pl.when(first_ref[w] == 1)
        def _():
            o_ref[...] = y

        @pl.when(first_ref[w] == 0)
        def _():
            o_ref[...] += y


def expert_ffn(xs, items, wg, wu, wd, layer):
    rows, dm = xs.shape
    de = wg.shape[-1]
    n_items = items[0].shape[0]
    n_prefetch = len(items)

    def row_map(w, tile, *_):
        return (tile[w], 0)

    hbm = pl.BlockSpec(memory_space=pl.ANY)
    return pl.pallas_call(
        functools.partial(_ffn_items_kernel, layer=layer),
        out_shape=jax.ShapeDtypeStruct((rows, dm), F32),
        grid_spec=pltpu.PrefetchScalarGridSpec(
            num_scalar_prefetch=n_prefetch,
            grid=(n_items,),
            in_specs=[pl.BlockSpec((MOE_TILE, dm), row_map), hbm, hbm, hbm],
            out_specs=pl.BlockSpec((MOE_TILE, dm), row_map),
            scratch_shapes=[pltpu.VMEM((2, dm, de), F32), pltpu.VMEM((2, dm, de), F32),
                            pltpu.VMEM((2, de, dm), F32), pltpu.VMEM((dm, de), BF16),
                            pltpu.VMEM((dm, de), BF16), pltpu.VMEM((de, dm), BF16),
                            pltpu.SemaphoreType.DMA((2,))]),
        compiler_params=pltpu.CompilerParams(
            dimension_semantics=("arbitrary",), vmem_limit_bytes=FFN_VMEM_LIMIT_BYTES),
        name="expert_ffn",
    )(*items, xs, wg, wu, wd)


def _ffn_dense_kernel(x_ref, wg_ref, wu_ref, wd_ref, o_ref):
    o_ref[...] = _ffn_body(x_ref[...], wg_ref, wu_ref, wd_ref, None, None)


def shared_ffn(x, wg, wu, wd):
    tt, dm = x.shape
    de = wg.shape[-1]
    row = pl.BlockSpec((MOE_TILE, dm), lambda i: (i, 0))
    return pl.pallas_call(
        _ffn_dense_kernel,
        out_shape=jax.ShapeDtypeStruct((tt, dm), F32),
        grid=(tt // MOE_TILE,),
        in_specs=[row, pl.BlockSpec((dm, de), lambda i: (0, 0)), pl.BlockSpec((dm, de), lambda i: (0, 0)),
                  pl.BlockSpec((de, dm), lambda i: (0, 0))],
        out_specs=row,
        compiler_params=pltpu.CompilerParams(
            dimension_semantics=("arbitrary",), vmem_limit_bytes=VMEM_LIMIT_BYTES),
        name="shared_ffn",
    )(x, wg, wu, wd)


def layer_norm(x, g, b):
    mu = jnp.mean(x, axis=-1, keepdims=True)
    var = jnp.mean(jnp.square(x - mu), axis=-1, keepdims=True)
    return (x - mu) * lax.rsqrt(var + LN_EPS) * g + b


def _ffn_items(counts, n_rows):
    n_tiles = n_rows // MOE_TILE
    n_items = n_tiles + N_EXPERTS - 1
    end = jnp.cumsum(counts)
    start = end - counts
    first_tile = start // MOE_TILE
    tiles_of = jnp.where(counts > 0, (end - 1) // MOE_TILE - first_tile + 1, 0)
    item_end = jnp.cumsum(tiles_of)
    item_start = item_end - tiles_of
    total = item_end[-1]
    w = jnp.minimum(jnp.arange(n_items, dtype=jnp.int32), total - 1)
    exp = jnp.sum((item_end[None, :] <= w[:, None]).astype(jnp.int32), axis=1)
    onehot = (exp[:, None] == jnp.arange(N_EXPERTS, dtype=jnp.int32)[None, :]).astype(jnp.int32)
    pick = lambda v: jnp.sum(onehot * v[None, :], axis=1)
    tile = pick(first_tile) + (w - pick(item_start))
    lo = jnp.maximum(pick(start) - tile * MOE_TILE, 0)
    hi = jnp.minimum(pick(end) - tile * MOE_TILE, MOE_TILE)
    prev_tile = jnp.concatenate([jnp.full((1,), -1, jnp.int32), tile[:-1]])
    first = (tile != prev_tile).astype(jnp.int32)
    prev_exp = jnp.concatenate([jnp.full((1,), -1, jnp.int32), exp[:-1].astype(jnp.int32)])
    newexp = (exp != prev_exp).astype(jnp.int32)
    parity = (jnp.cumsum(newexp) - 1) % 2
    ids = jnp.arange(N_EXPERTS, dtype=jnp.int32)
    later = (ids[None, :] > ids[:, None]) & (counts[None, :] > 0)
    next_of = jnp.min(jnp.where(later, ids[None, :], N_EXPERTS), axis=1)
    next_of = jnp.where(next_of == N_EXPERTS, -1, next_of)
    i32 = lambda v: v.astype(jnp.int32)
    return start.astype(jnp.int32), (i32(tile), i32(exp), i32(lo), i32(hi), first, i32(total).reshape(1),
                                     newexp, i32(pick(next_of)), i32(parity))


def moe_sublayer(x, t, mods, ln_g, ln_b, layout, layer, w_router, router_bias, w_gate, w_up, w_down,
                 ws_gate, ws_up, ws_down):
    tt, dm = t.shape
    eidx, pos, gate, counts = route(t, w_router, router_bias)
    start, items = _ffn_items(counts, tt * TOP_K)
    ids = jnp.arange(N_EXPERTS, dtype=jnp.int32)
    slot = pos + jnp.sum(jnp.where(eidx[..., None] == ids, start, 0), axis=-1)
    xs = dispatch(t, slot)
    ys = expert_ffn(xs, items, w_gate, w_up, w_down, layer)
    shared = shared_ffn(t, ws_gate, ws_up, ws_down)
    gate_cols = jnp.swapaxes(gate, 1, 2).reshape(tt, 8)
    return combine(shared, ys, gate_cols, slot, x, mods, ln_g, ln_b, layout)


TOKEN_TILE = 256
MOD_SHIFT1, MOD_SCALE1, MOD_GATE1, MOD_SHIFT2, MOD_SCALE2, MOD_GATE2 = range(6)


class Layout:
    def __init__(self, n_ctx_seq, ctx_len, n_lat_seq, lat_len):
        self.n_ctx_seq, self.ctx_len, self.n_lat_seq, self.lat_len = n_ctx_seq, ctx_len, n_lat_seq, lat_len
        self.n_ctx = n_ctx_seq * ctx_len
        self.n_tok = self.n_ctx + n_lat_seq * lat_len
        assert ctx_len % TOKEN_TILE == 0 and lat_len % TOKEN_TILE == 0

    def cond_row(self, tile):
        ctx_tiles = self.n_ctx // TOKEN_TILE
        return jnp.where(tile < ctx_tiles, 0, 1 + (tile - ctx_tiles) // (self.lat_len // TOKEN_TILE))

    def tile_pos(self, tile):
        ctx_tiles = self.n_ctx // TOKEN_TILE
        per_ctx, per_lat = self.ctx_len // TOKEN_TILE, self.lat_len // TOKEN_TILE
        is_ctx = tile < ctx_tiles
        return (jnp.where(is_ctx, tile % per_ctx, (tile - ctx_tiles) % per_lat),
                jnp.where(is_ctx, per_ctx, per_lat))


def _premix_kernel(x_ref, prev_ref, next_ref, mod_ref, mp_ref, mn_ref, *out_refs, layout):
    tm = x_ref.shape[0]
    pos, per_seq = layout.tile_pos(pl.program_id(0))
    shift, scale = mod_ref[MOD_SHIFT1:MOD_SHIFT1 + 1, :], mod_ref[MOD_SCALE1:MOD_SCALE1 + 1, :]
    mod = lambda t: t * (1.0 + scale) + shift
    h = mod(x_ref[...])
    h_before = jnp.where(pos > 0, mod(prev_ref[...]), 0.0)
    h_after = jnp.where(pos < per_seq - 1, mod(next_ref[...]), 0.0)
    row = lax.broadcasted_iota(jnp.int32, h.shape, 0)
    d_prev = jnp.where(row == 0, h_before, pltpu.roll(h, 1, 0)) - h
    d_next = jnp.where(row == tm - 1, h_after, pltpu.roll(h, tm - 1, 0)) - h
    for i, o_ref in enumerate(out_refs):
        o_ref[...] = (h + mp_ref[i:i + 1, :] * d_prev + mn_ref[i:i + 1, :] * d_next).astype(BF16)


def premix(x, mods, mix_prev, mix_next, layout):
    tt, dm = x.shape
    nt = tt // TOKEN_TILE
    zero = jnp.zeros((1, dm), F32)
    prev_rows = jnp.concatenate([zero, x[TOKEN_TILE - 1::TOKEN_TILE][:-1]], axis=0)[:, None, :]
    next_rows = jnp.concatenate([x[::TOKEN_TILE][1:], zero], axis=0)[:, None, :]
    tile = pl.BlockSpec((TOKEN_TILE, dm), lambda i: (i, 0))
    edge = pl.BlockSpec((None, 1, dm), lambda i: (i, 0, 0))
    par = pl.BlockSpec((6, dm), lambda i: (0, 0))
    return pl.pallas_call(
        functools.partial(_premix_kernel, layout=layout),
        out_shape=[jax.ShapeDtypeStruct((tt, dm), BF16)] * 6,
        grid=(nt,),
        in_specs=[tile, edge, edge, pl.BlockSpec((None, 6, dm), lambda i: (layout.cond_row(i), 0, 0)), par, par],
        out_specs=[tile] * 6,
        compiler_params=pltpu.CompilerParams(
            dimension_semantics=("arbitrary",), vmem_limit_bytes=VMEM_LIMIT_BYTES),
        name="premix",
    )(x, prev_rows, next_rows, mods, mix_prev, mix_next)


def _residual_ln(x, y, gate, g, b):
    z = ALPHA * x + gate * y
    mu = jnp.mean(z, axis=-1, keepdims=True)
    zc = z - mu
    var = jnp.mean(zc * zc, axis=-1, keepdims=True)
    return zc * lax.rsqrt(var + LN_EPS) * g + b


def _post_kernel(x_ref, y_ref, mod_ref, g_ref, b_ref, *out_refs, gate_row, mod_rows):
    xn = _residual_ln(x_ref[...], y_ref[...], mod_ref[gate_row:gate_row + 1, :], g_ref[...], b_ref[...])
    out_refs[0][...] = xn
    if mod_rows is not None:
        sh, sc = mod_rows
        out_refs[1][...] = xn * (1.0 + mod_ref[sc:sc + 1, :]) + mod_ref[sh:sh + 1, :]


def post(x, y, mods, ln_g, ln_b, layout, gate_row, mod_rows):
    tt, dm = x.shape
    tile = pl.BlockSpec((TOKEN_TILE, dm), lambda i: (i, 0))
    vec = pl.BlockSpec((1, dm), lambda i: (0, 0))
    n_out = 1 if mod_rows is None else 2
    return pl.pallas_call(
        functools.partial(_post_kernel, gate_row=gate_row, mod_rows=mod_rows),
        out_shape=[jax.ShapeDtypeStruct((tt, dm), F32)] * n_out,
        grid=(tt // TOKEN_TILE,),
        in_specs=[tile, tile, pl.BlockSpec((None, 6, dm), lambda i: (layout.cond_row(i), 0, 0)), vec, vec],
        out_specs=[tile] * n_out,
        compiler_params=pltpu.CompilerParams(
            dimension_semantics=("arbitrary",), vmem_limit_bytes=VMEM_LIMIT_BYTES),
        name="post",
    )(x, y, mods, ln_g[None, :], ln_b[None, :])


def _pool_kernel(x_ref, mod_ref, w_ref, scale_ref, o_ref, *, grid_w):
    seq_len = x_ref.shape[0]
    t_idx = lax.broadcasted_iota(jnp.int32, x_ref.shape, 0)

    def shifted(v, k, pos, extent, stride):
        if k > 0:
            return jnp.where(pos < extent - k, pltpu.roll(v, seq_len - k * stride, 0), 0.0)
        return jnp.where(pos >= -k, pltpu.roll(v, -k * stride, 0), 0.0)

    def box(v, w, pos, extent, stride):
        m = w // 2
        ahead, behind, step = v, v, 1
        while step < m:
            ahead = ahead + shifted(ahead, step, pos, extent, stride)
            behind = behind + shifted(behind, -step, pos, extent, stride)
            step *= 2
        total = ahead + shifted(behind, -1, pos, extent, stride)
        cnt = jnp.minimum(pos + m, extent) - jnp.maximum(pos - m, 0)
        return total, cnt.astype(F32)

    h = x_ref[...] * (1.0 + mod_ref[MOD_SCALE1:MOD_SCALE1 + 1, :]) + mod_ref[MOD_SHIFT1:MOD_SHIFT1 + 1, :]
    for gi, w in enumerate(POOL_WINDOWS):
        @pl.when(pl.program_id(1) == gi)
        def _(w=w):
            if grid_w is None:
                s, cnt = box(h, w, t_idx, seq_len, 1)
            else:
                s, cr = box(h, w, t_idx // grid_w, seq_len // grid_w, grid_w)
                s, cc = box(s, w, t_idx % grid_w, grid_w, 1)
                cnt = cr * cc
            p = (s / cnt - h).astype(BF16)
            o_ref[...] = _dot(p, w_ref[...].astype(BF16)) * scale_ref[...]


def pool_mix(x, mods, pool_w, pool_scale, layout, latent):
    dm = x.shape[1]
    n_grp = len(POOL_WINDOWS)
    if latent:
        n_seq, seq_len, first_blk, cond0 = layout.n_lat_seq, layout.lat_len, layout.n_ctx // layout.lat_len, 1
    else:
        n_seq, seq_len, first_blk, cond0 = layout.n_ctx_seq, layout.ctx_len, 0, None
    cond = (lambda i: 0) if cond0 is None else (lambda i: cond0 + i)
    return pl.pallas_call(
        functools.partial(_pool_kernel, grid_w=GRID_W if latent else None),
        out_shape=jax.ShapeDtypeStruct((n_seq * seq_len, dm), F32),
        grid=(n_seq, n_grp),
        in_specs=[pl.BlockSpec((seq_len, POOL_GROUP), lambda i, g: (first_blk + i, g)),
                  pl.BlockSpec((None, 6, POOL_GROUP), lambda i, g: (cond(i), 0, g)),
                  pl.BlockSpec((None, POOL_GROUP, POOL_GROUP), lambda i, g: (g, 0, 0)),
                  pl.BlockSpec((1, POOL_GROUP), lambda i, g: (0, g))],
        out_specs=pl.BlockSpec((seq_len, POOL_GROUP), lambda i, g: (i, g)),
        compiler_params=pltpu.CompilerParams(
            dimension_semantics=("arbitrary", "arbitrary"), vmem_limit_bytes=VMEM_LIMIT_BYTES),
        name="pool_mix",
    )(x, mods, pool_w, pool_scale[None, :])


def kernel(x_prompt, x_sample, c, state_rwkv, c_ctx, w_ada, b_ada, ln_g, ln_b, rw_mix_prev, rw_mix_next, rw_w_r, rw_w_k, rw_w_v, rw_w_o, rw_w0, rw_w1, rw_w2, rw_a0, rw_a1, rw_a2, rw_g1, rw_g2, rw_k_k, rw_k_a, rw_r_k, rw_gn_g, rw_gn_b, pool_w, pool_scale, moe_router, moe_router_bias, moe_w_gate, moe_w_up, moe_w_down, moe_ws_gate, moe_ws_up, moe_ws_down):
    dm = D_MODEL
    H, N = RWKV_HEADS, RWKV_HEAD
    bc, lc_, _ = x_prompt.shape
    bl, ll, _ = x_sample.shape
    layout = Layout(bc, lc_, bl, ll)
    tc = layout.n_ctx
    x = jnp.concatenate([x_prompt.reshape(tc, dm), x_sample.reshape(bl * ll, dm)], axis=0)
    cond = jnp.concatenate([c_ctx[None, :], c], axis=0)
    n_cond = cond.shape[0]
    assert n_cond <= 8
    cond_pad = jnp.zeros((8, dm), F32).at[:n_cond].set(jax.nn.silu(cond))
    new_states = []

    for l in range(DEPTH):
        mods = (matmul(cond_pad, w_ada, tn=1024, layer=l) + b_ada[l]).reshape(8, 6, dm)
        j = l // 2
        if l % 2 == 0:
            xr, xw, xk, xv, xa, xg = premix(x, mods, rw_mix_prev[j], rw_mix_next[j], layout)
            r = matmul(xr, rw_w_r[j])
            k = matmul(xk, rw_w_k[j])
            v = matmul(xv, rw_w_v[j])
            g = matmul(jax.nn.sigmoid(matmul(xg, rw_g1[j])), rw_g2[j])
            wl = [matmul(jnp.tanh(matmul(xw, rw_w1[j, d])), rw_w2[j, d]) for d in range(2)]
            al = [matmul(matmul(xa, rw_a1[j, d]), rw_a2[j, d]) for d in range(2)]
            par = jnp.zeros((16, dm), F32).at[:9].set(jnp.stack(
                [rw_w0[j, 0], rw_w0[j, 1], rw_a0[j, 0], rw_a0[j, 1], rw_k_k[j], rw_k_a[j],
                 rw_r_k[j].reshape(dm), rw_gn_g[j], rw_gn_b[j]]))
            out_c, s_fin = rwkv_core(r, k, v, g, wl, al, par, jnp.zeros((bc, 2, H, N, N), F32), 0, lc_)
            out_l, _ = rwkv_core(r, k, v, g, wl, al, par, state_rwkv[:, j], tc, ll)
            new_states.append(s_fin)
            mix = matmul(jnp.concatenate([out_c, out_l], axis=0), rw_w_o[j])
        else:
            mix = jnp.concatenate([pool_mix(x, mods, pool_w[j], pool_scale[j], layout, False),
                                   pool_mix(x, mods, pool_w[j], pool_scale[j], layout, True)], axis=0)
        x, h2 = post(x, mix, mods, ln_g[l, 0], ln_b[l, 0], layout, MOD_GATE1, (MOD_SHIFT2, MOD_SCALE2))
        x = moe_sublayer(x, h2, mods, ln_g[l, 1], ln_b[l, 1], layout, l, moe_router[l], moe_router_bias[l],
                         moe_w_gate, moe_w_up, moe_w_down, moe_ws_gate[l], moe_ws_up[l], moe_ws_down[l])

    return (x[:tc].reshape(bc, lc_, dm), x[tc:].reshape(bl, ll, dm), jnp.stack(new_states, axis=1))
```

```python
import functools

import jax
import jax.numpy as jnp
from jax import lax
from jax.experimental import pallas as pl
from jax.experimental.pallas import tpu as pltpu

D_MODEL = 2048
DEPTH = 2
GRID_W = 64
RWKV_HEAD = 64
RWKV_HEADS = D_MODEL // RWKV_HEAD
POOL_WINDOWS = (2, 4, 8, 16)
POOL_GROUP = D_MODEL // len(POOL_WINDOWS)
N_EXPERTS = 64
TOP_K = 6
N_GROUPS = 8
TOPK_GROUPS = 4
ROUTED_SCALE = 2.5
ALPHA = (2 * DEPTH) ** 0.25
LN_EPS = 1e-5
GN_EPS = 64e-5

F32 = jnp.float32
BF16 = jnp.bfloat16

VMEM_LIMIT_BYTES = 48 * 1024 * 1024
FFN_VMEM_LIMIT_BYTES = 56 * 1024 * 1024

SCAN_CHUNK = 64
SCAN_HEADS = 4
SCAN_LANES = SCAN_HEADS * RWKV_HEAD
SCAN_PREP_CHUNKS = 4
MOE_TILE = 256


def _dot(a, b):
    return jnp.dot(a, b, preferred_element_type=F32)


def _dot_nt(a, b):
    return lax.dot_general(a, b, (((1,), (1,)), ((), ())), preferred_element_type=F32)


def _dot_tn(a, b):
    return lax.dot_general(a, b, (((0,), (0,)), ((), ())), preferred_element_type=F32)


def _mm_kernel(x_ref, w_ref, o_ref):
    o_ref[...] = _dot(x_ref[...].astype(BF16), w_ref[...].astype(BF16))


def matmul(x, w, *, tm=1024, tn=512, layer=None):
    m, k = x.shape
    n = w.shape[-1]
    tm = min(tm, m)
    tn = min(tn, n)
    assert m % tm == 0 and n % tn == 0
    if layer is None:
        w_spec = pl.BlockSpec((k, tn), lambda j, i: (0, j))
    else:
        w_spec = pl.BlockSpec((None, k, tn), lambda j, i: (layer, 0, j))
    return pl.pallas_call(
        _mm_kernel,
        out_shape=jax.ShapeDtypeStruct((m, n), F32),
        grid=(n // tn, m // tm),
        in_specs=[pl.BlockSpec((tm, k), lambda j, i: (i, 0)), w_spec],
        out_specs=pl.BlockSpec((tm, tn), lambda j, i: (i, j)),
        compiler_params=pltpu.CompilerParams(
            dimension_semantics=("arbitrary", "arbitrary"),
            vmem_limit_bytes=VMEM_LIMIT_BYTES),
        name="matmul",
    )(x, w)


def _lora_kernel(x_ref, w1_ref, w2_ref, o_ref, *, act):
    h = _dot(x_ref[...].astype(BF16), w1_ref[...].astype(BF16))
    if act is not None:
        h = act(h)
    o_ref[...] = _dot(h.astype(BF16), w2_ref[...].astype(BF16))


def lora(x, w1, w2, act=None, *, tm=1024):
    m, k = x.shape
    r = w1.shape[1]
    n = w2.shape[1]
    assert m % tm == 0
    return pl.pallas_call(
        functools.partial(_lora_kernel, act=act),
        out_shape=jax.ShapeDtypeStruct((m, n), F32),
        grid=(m // tm,),
        in_specs=[pl.BlockSpec((tm, k), lambda i: (i, 0)), pl.BlockSpec((k, r), lambda i: (0, 0)),
                  pl.BlockSpec((r, n), lambda i: (0, 0))],
        out_specs=pl.BlockSpec((tm, n), lambda i: (i, 0)),
        compiler_params=pltpu.CompilerParams(
            dimension_semantics=("arbitrary",), vmem_limit_bytes=VMEM_LIMIT_BYTES),
        name="lora",
    )(x, w1, w2)


def _split3(x):
    hi = x.astype(BF16)
    r1 = x - hi.astype(F32)
    mid = r1.astype(BF16)
    lo = (r1 - mid.astype(F32)).astype(BF16)
    return hi, mid, lo


def _softplus(z):
    return jnp.maximum(z, 0.0) + jnp.log(1.0 + jnp.exp(-jnp.abs(z)))


def _scan_kernel(r_ref, k_ref, v_ref, g_ref, wl0_ref, wl1_ref, al0_ref, al1_ref, par_ref, s0_ref,
                 out_ref, sfin_ref, s_scr, kk_ref, lw_ref, a_ref, kd_ref, bonus_ref, y_ref,
                 pr_s, ak_s, lra_s, uy_s, gend_s, lvl_s, *, nc, prep):
    C, N, HB, W = SCAN_CHUNK, RWKV_HEAD, SCAN_HEADS, SCAN_LANES
    t_idx = lax.broadcasted_iota(jnp.int32, (C, W), 0)
    s_idx = lax.broadcasted_iota(jnp.int32, (C, W), 1) % C
    eye_row = (s_idx == t_idx).astype(F32)
    tri_t = lax.broadcasted_iota(jnp.int32, (C, C), 0)
    tri_s = lax.broadcasted_iota(jnp.int32, (C, C), 1)
    blk_r = lax.broadcasted_iota(jnp.int32, (W, W), 0) // N
    blk_c = lax.broadcasted_iota(jnp.int32, (W, W), 1) // N
    bd_mask = blk_r == blk_c
    bd_mask_bf = bd_mask.astype(BF16)

    def expand(x_row_bf):
        return jnp.concatenate([x_row_bf] * HB, axis=0) * bd_mask_bf

    for d in range(2):
        s_scr[d] = jnp.zeros((W, W), F32)
        for h in range(HB):
            s_scr[d, h * N:(h + 1) * N, h * N:(h + 1) * N] = s0_ref[d, h]
    y_ref[...] = jnp.zeros_like(y_ref)

    levels = []
    b = 2
    while b < C:
        levels.append(b)
        b *= 2

    @pl.when((pl.program_id(0) == 0) & (pl.program_id(1) == 0))
    def _():
        t_bd = lax.broadcasted_iota(jnp.int32, (W, W), 0) % C
        s_bd = lax.broadcasted_iota(jnp.int32, (W, W), 1) % C
        for li, b in enumerate(levels):
            tb, sb = t_bd // b, s_bd // b
            lvl_s[0, li] = (bd_mask & (tb % 2 == 1) & (sb == tb - 1)).astype(BF16)
            lvl_s[1, li] = (bd_mask & (tb % 2 == 0) & (sb == tb + 1)).astype(BF16)

    ones_bd = bd_mask_bf
    RB = 4 * C
    par = lambda i: par_ref[i:i + 1, :]

    def head_sum(x):
        hi, lo = _split2(x)
        return _dot(hi, ones_bd) + _dot(lo, ones_bd)

    def prologue(i, carry):
        rows = pl.ds(pl.multiple_of(i * RB, RB), RB)
        k = k_ref[rows, :]
        kkp = k * par(4)
        kk_ref[rows, :] = kkp * lax.rsqrt(jnp.maximum(head_sum(kkp * kkp), 1e-24))
        kd_sum = jnp.zeros_like(k)
        for d, (wl_ref, al_ref) in enumerate(((wl0_ref, al0_ref), (wl1_ref, al1_ref))):
            w_log = -_softplus(-(par(d) + wl_ref[rows, :])) - 0.5
            lw_ref[d, rows, :] = -jnp.exp(w_log)
            a = jax.nn.sigmoid(par(2 + d) + al_ref[rows, :])
            a_ref[d, rows, :] = a
            kd = k * (1.0 + (a - 1.0) * par(5))
            kd_ref[d, rows, :] = kd
            kd_sum = kd_sum + kd
        bonus_ref[rows, :] = head_sum(r_ref[rows, :] * kd_sum * par(6)) * v_ref[rows, :]
        return carry

    lax.fori_loop(0, nc * C // RB, prologue, 0)

    def interleave(chains):
        live = list(chains)
        while live:
            nxt = []
            for ch in live:
                try:
                    next(ch)
                    nxt.append(ch)
                except StopIteration:
                    pass
            live = nxt

    def prepare_chunk(d, ci):
        fwd = d == 0
        strict = (s_idx < t_idx) if fwd else (s_idx > t_idx)
        incl = (s_idx <= t_idx) if fwd else (s_idx >= t_idx)
        tri = ((tri_s <= tri_t) if fwd else (tri_s >= tri_t)).astype(BF16)
        rows = pl.ds(pl.multiple_of(ci * C, C), C)
        lw = lw_ref[d, rows, :]
        hi, mid, lo = _split3(lw)
        lc = _dot(tri, hi) + _dot(tri, mid) + _dot(tri, lo)
        yield
        kk = kk_ref[rows, :]
        g_in = jnp.exp(lc)
        g_inv = jnp.exp(-lc)
        g_ex = jnp.exp(lc - lw)
        last = C - 1 if fwd else 0
        gend_s[d, ci] = jnp.broadcast_to(g_in[last:last + 1, :], (8, W))
        bt = (kk * g_ex).astype(BF16)
        rt = (r_ref[rows, :] * g_in).astype(BF16)
        at = (-(kk * a_ref[d, rows, :]) * g_inv).astype(BF16)
        kt = (kd_ref[d, rows, :] * g_inv).astype(BF16)
        br = jnp.concatenate([bt, rt], axis=0)
        ak_s[d, ci] = jnp.concatenate([at, kt], axis=0)
        la = _dot_nt(br, expand(at))
        lk = _dot_nt(br, expand(kt))
        yield
        lba = la[:C]
        lra_s[d, ci] = jnp.where(incl, la[C:], 0.0).astype(BF16)
        lbrk = jnp.concatenate([jnp.where(strict, lk[:C], 0.0), jnp.where(incl, lk[C:], 0.0)], axis=0)
        wy2 = _dot(lbrk.astype(BF16), expand(v_ref[rows, :].astype(BF16)))

        pair = (s_idx == t_idx - 1) if fwd else (s_idx == t_idx + 1)
        inv = eye_row + jnp.where(pair & (t_idx % 2 == (1 if fwd else 0)), lba, 0.0)
        lba4 = jnp.concatenate([lba.astype(BF16)] * HB, axis=0)
        for li in range(len(levels)):
            m1 = _dot(inv.astype(BF16), lba4 * lvl_s[d, li])
            yield
            inv = inv + _dot(m1.astype(BF16), expand(inv.astype(BF16)))
            yield
        inv_bf = inv.astype(BF16)
        pu = _dot(inv_bf, expand(bt))
        u0 = _dot(inv_bf, expand(wy2[:C].astype(BF16)))
        yield
        pr_s[d, ci] = jnp.concatenate([pu.astype(BF16), rt], axis=0)
        uy_s[d, ci] = jnp.concatenate([u0, wy2[C:]], axis=0)

    def prepare(i, carry):
        interleave([prepare_chunk(d, i * prep + j) for j in range(prep) for d in range(2)])
        return carry

    lax.fori_loop(0, nc // prep, prepare, 0)

    def advance_chunk(d, ci):
        rows = pl.ds(pl.multiple_of(ci * C, C), C)
        s_bd = s_scr[d]
        py = _dot_nt(pr_s[d, ci], s_bd.astype(BF16))
        yield
        uy = uy_s[d, ci]
        u_bf = (py[:C] + uy[:C]).astype(BF16)
        upd = _dot_tn(jnp.concatenate([u_bf, v_ref[rows, :].astype(BF16)], axis=0), ak_s[d, ci])
        y = py[C:] + uy[C:] + _dot(lra_s[d, ci], expand(u_bf))
        yield
        y_ref[rows, :] += y
        s_scr[d] = jnp.where(bd_mask, (s_bd + upd) * gend_s[d, ci][0:1, :], 0.0)

    def advance(c, carry):
        interleave([advance_chunk(0, c), advance_chunk(1, nc - 1 - c)])
        return carry

    lax.fori_loop(0, nc, advance, 0)

    for d in range(2):
        for h in range(HB):
            sfin_ref[d, h] = s_scr[d, h * N:(h + 1) * N, h * N:(h + 1) * N]

    def epilogue(i, carry):
        rows = pl.ds(pl.multiple_of(i * RB, RB), RB)
        y = y_ref[rows, :]
        yc = y - head_sum(y) * (1.0 / N)
        var = head_sum(yc * yc) * (1.0 / N)
        yn = yc * lax.rsqrt(var + GN_EPS) * par(7) + par(8)
        out_ref[rows, :] = ((yn + bonus_ref[rows, :]) * g_ref[rows, :]).astype(BF16)
        return carry

    lax.fori_loop(0, nc * C // RB, epilogue, 0)


def rwkv_core(r, k, v, g, wl, al, par, s0, first_row, seq_len):
    b = s0.shape[0]
    d = r.shape[1]
    l = seq_len
    assert first_row % l == 0
    nc = l // SCAN_CHUNK
    first_blk = first_row // l
    seq = pl.BlockSpec((l, SCAN_LANES), lambda i, j: (first_blk + i, j))
    oseq = pl.BlockSpec((l, SCAN_LANES), lambda i, j: (i, j))
    st = pl.BlockSpec((None, 2, SCAN_HEADS, RWKV_HEAD, RWKV_HEAD), lambda i, j: (i, 0, j, 0, 0))
    seq_scr = pltpu.VMEM((l, SCAN_LANES), F32)
    seq2_scr = pltpu.VMEM((2, l, SCAN_LANES), F32)
    C, W = SCAN_CHUNK, SCAN_LANES
    per_chunk = lambda rows, dt: pltpu.VMEM((2, nc, rows, W), dt)
    return pl.pallas_call(
        functools.partial(_scan_kernel, nc=nc, prep=SCAN_PREP_CHUNKS),
        out_shape=(jax.ShapeDtypeStruct((b * l, d), BF16), jax.ShapeDtypeStruct(s0.shape, F32)),
        grid=(b, d // SCAN_LANES),
        in_specs=[seq] * 8 + [pl.BlockSpec((16, SCAN_LANES), lambda i, j: (0, j)), st],
        out_specs=(oseq, st),
        scratch_shapes=[pltpu.VMEM((2, SCAN_LANES, SCAN_LANES), F32),
                        seq_scr, seq2_scr, seq2_scr, seq2_scr, seq_scr, seq_scr,
                        per_chunk(2 * C, BF16), per_chunk(2 * C, BF16), per_chunk(C, BF16),
                        per_chunk(2 * C, F32), per_chunk(8, F32),
                        pltpu.VMEM((2, 5, W, W), BF16)],
        compiler_params=pltpu.CompilerParams(
            dimension_semantics=("arbitrary", "arbitrary"),
            vmem_limit_bytes=VMEM_LIMIT_BYTES),
        name="wkv_scan",
    )(r, k, v, g, wl[0], wl[1], al[0], al[1], par, s0)


def _split2(x):
    hi = x.astype(BF16)
    return hi, (x - hi.astype(F32)).astype(BF16)


def _first_index(hit_value, iota, sentinel, axis):
    return jnp.min(jnp.where(hit_value, iota, sentinel), axis=axis, keepdims=True)


def _router_kernel(h_ref, wrt_ref, bias_ref, eidx_ref, pos_ref, gate_ref, cnt_ref, cnt_scr):
    tm = h_ref.shape[0]
    E, G, GS = N_EXPERTS, N_GROUPS, N_EXPERTS // N_GROUPS
    neg = -jnp.inf

    @pl.when(pl.program_id(0) == 0)
    def _():
        cnt_scr[...] = jnp.zeros_like(cnt_scr)

    x_hi, x_lo = _split2(h_ref[...])
    w_hi, w_lo = _split2(wrt_ref[...])
    logits = _dot_nt(w_hi, x_hi) + (_dot_nt(w_hi, x_lo) + _dot_nt(w_lo, x_hi))
    scores = jax.nn.sigmoid(logits)
    biased = scores + bias_ref[...]

    g3 = biased.reshape(G, GS, tm)
    w_iota = lax.broadcasted_iota(jnp.int32, (G, GS, tm), 1).astype(F32)
    m1 = jnp.max(g3, axis=1, keepdims=True)
    first = _first_index(g3 == m1, w_iota, float(GS), 1)
    m2 = jnp.max(jnp.where(w_iota == first, neg, g3), axis=1, keepdims=True)
    gs = (m1 + m2).reshape(G, tm)

    g_iota = lax.broadcasted_iota(jnp.int32, (G, tm), 0).astype(F32)
    gsel = jnp.zeros((G, tm), jnp.bool_)
    cur = gs
    for _ in range(TOPK_GROUPS):
        m = jnp.max(cur, axis=0, keepdims=True)
        hit = g_iota == _first_index(cur == m, g_iota, float(G), 0)
        gsel = gsel | hit
        cur = jnp.where(hit, neg, cur)

    masked = jnp.where(gsel[:, None, :], g3, neg).reshape(E, tm)
    e_iota = lax.broadcasted_iota(jnp.int32, (E, tm), 0).astype(F32)
    sel = jnp.zeros((E, tm), jnp.bool_)
    hits, ids = [], []
    cur = masked
    for _ in range(TOP_K):
        m = jnp.max(cur, axis=0, keepdims=True)
        f = _first_index(cur == m, e_iota, float(E), 0)
        hit = e_iota == f
        hits.append(hit)
        ids.append(f)
        sel = sel | hit
        cur = jnp.where(hit, neg, cur)

    ssum = jnp.sum(jnp.where(sel, scores, 0.0), axis=0, keepdims=True)
    gates = scores / (ssum + 1e-20) * ROUTED_SCALE

    before = (lax.broadcasted_iota(jnp.int32, (tm, tm), 0) < lax.broadcasted_iota(jnp.int32, (tm, tm), 1))
    sel_f = sel.astype(F32)
    rank = _dot(sel_f.astype(BF16), before.astype(BF16)) + cnt_scr[:, 0:1]
    cnt_new = cnt_scr[...] + jnp.sum(sel_f, axis=1, keepdims=True)
    cnt_scr[...] = cnt_new
    cnt_ref[...] = cnt_new.astype(jnp.int32)

    k_iota = lax.broadcasted_iota(jnp.int32, (8, tm), 0)
    eidx8 = jnp.zeros((8, tm), F32)
    pos8 = jnp.zeros((8, tm), F32)
    gate8 = jnp.zeros((8, tm), F32)
    for k in range(TOP_K):
        row = k_iota == k
        eidx8 = jnp.where(row, ids[k], eidx8)
        pos8 = jnp.where(row, jnp.sum(jnp.where(hits[k], rank, 0.0), axis=0, keepdims=True), pos8)
        gate8 = jnp.where(row, jnp.sum(jnp.where(hits[k], gates, 0.0), axis=0, keepdims=True), gate8)
    eidx_ref[...] = eidx8.astype(jnp.int32)
    pos_ref[...] = pos8.astype(jnp.int32)
    gate_ref[...] = gate8


def route(h, w_router, router_bias):
    tt, dm = h.shape
    nt = tt // MOE_TILE
    tok = pl.BlockSpec((None, 8, MOE_TILE), lambda i: (i, 0, 0))
    eidx, pos, gate, cnt = pl.pallas_call(
        _router_kernel,
        out_shape=(jax.ShapeDtypeStruct((nt, 8, MOE_TILE), jnp.int32),
                   jax.ShapeDtypeStruct((nt, 8, MOE_TILE), jnp.int32),
                   jax.ShapeDtypeStruct((nt, 8, MOE_TILE), F32),
                   jax.ShapeDtypeStruct((N_EXPERTS, 128), jnp.int32)),
        grid=(nt,),
        in_specs=[pl.BlockSpec((MOE_TILE, dm), lambda i: (i, 0)),
                  pl.BlockSpec((N_EXPERTS, dm), lambda i: (0, 0)),
                  pl.BlockSpec((N_EXPERTS, 1), lambda i: (0, 0))],
        out_specs=(tok, tok, tok, pl.BlockSpec((N_EXPERTS, 128), lambda i: (0, 0))),
        scratch_shapes=[pltpu.VMEM((N_EXPERTS, 128), F32)],
        compiler_params=pltpu.CompilerParams(
            dimension_semantics=("arbitrary",), vmem_limit_bytes=VMEM_LIMIT_BYTES),
        name="moe_router",
    )(h, w_router.T, router_bias[:, None])
    return eidx, pos, gate, cnt[:, 0]


def _dispatch_kernel(slot_ref, x_ref, xs_ref, sem):
    tm = x_ref.shape[0]

    def issue(t, carry):
        for k in range(TOP_K):
            pltpu.make_async_copy(x_ref.at[pl.ds(t, 1), :], xs_ref.at[pl.ds(slot_ref[k, t], 1), :],
                                  sem).start(priority=k % 2)
        return carry

    lax.fori_loop(0, tm, issue, 0)
    for k in range(TOP_K):
        pltpu.make_async_copy(x_ref, xs_ref.at[pl.ds(0, tm), :], sem).wait()


def dispatch(h, slot):
    tt, dm = h.shape
    nt = tt // MOE_TILE
    return pl.pallas_call(
        _dispatch_kernel,
        out_shape=jax.ShapeDtypeStruct((tt * TOP_K, dm), F32),
        grid=(nt,),
        in_specs=[pl.BlockSpec((None, 8, MOE_TILE), lambda i: (i, 0, 0), memory_space=pltpu.SMEM),
                  pl.BlockSpec((MOE_TILE, dm), lambda i: (i, 0))],
        out_specs=pl.BlockSpec(memory_space=pl.ANY),
        scratch_shapes=[pltpu.SemaphoreType.DMA],
        compiler_params=pltpu.CompilerParams(
            dimension_semantics=("arbitrary",), vmem_limit_bytes=VMEM_LIMIT_BYTES),
        name="moe_dispatch",
    )(slot, h)


def _combine_kernel(slot_ref, slot_nxt_ref, gate_ref, base_ref, x_ref,
                    mod_ref, g_ref, b_ref, ys_ref, o_ref, buf, sems):
    tm = base_ref.shape[0]
    i = pl.program_id(0)
    n = pl.num_programs(0)

    def gather(s_ref, slot_buf):
        def issue(t, carry):
            for k in range(TOP_K):
                pltpu.make_async_copy(ys_ref.at[pl.ds(s_ref[k, t], 1), :], buf.at[slot_buf, k, pl.ds(t, 1), :],
                                      sems.at[slot_buf]).start(priority=k % 2)
            return carry
        lax.fori_loop(0, tm, issue, 0)

    cur = i % 2
    for slot_buf in range(2):
        @pl.when((i == 0) & (cur == slot_buf))
        def _(slot_buf=slot_buf):
            gather(slot_ref, slot_buf)

        @pl.when((i + 1 < n) & (cur != slot_buf))
        def _(slot_buf=slot_buf):
            gather(slot_nxt_ref, slot_buf)

    for slot_buf in range(2):
        @pl.when(cur == slot_buf)
        def _(slot_buf=slot_buf):
            for k in range(TOP_K):
                pltpu.make_async_copy(ys_ref.at[pl.ds(0, tm), :], buf.at[slot_buf, k], sems.at[slot_buf]).wait()
            acc = base_ref[...]
            for k in range(TOP_K):
                acc = acc + gate_ref[:, k:k + 1] * buf[slot_buf, k]
            o_ref[...] = _residual_ln(x_ref[...], acc, mod_ref[MOD_GATE2:MOD_GATE2 + 1, :], g_ref[...], b_ref[...])


def combine(base, ys, gate_cols, slot, x, mods, ln_g, ln_b, layout):
    tt, dm = base.shape
    nt = tt // MOE_TILE
    smem_tok = pl.BlockSpec((None, 8, MOE_TILE), lambda i: (i, 0, 0), memory_space=pltpu.SMEM)
    smem_nxt = pl.BlockSpec((None, 8, MOE_TILE), lambda i: (jnp.minimum(i + 1, nt - 1), 0, 0),
                            memory_space=pltpu.SMEM)
    row = pl.BlockSpec((MOE_TILE, dm), lambda i: (i, 0))
    vec = pl.BlockSpec((1, dm), lambda i: (0, 0))
    return pl.pallas_call(
        _combine_kernel,
        out_shape=jax.ShapeDtypeStruct((tt, dm), F32),
        grid=(nt,),
        in_specs=[smem_tok, smem_nxt, pl.BlockSpec((MOE_TILE, 8), lambda i: (i, 0)), row, row,
                  pl.BlockSpec((None, 6, dm), lambda i: (layout.cond_row(i), 0, 0)), vec, vec,
                  pl.BlockSpec(memory_space=pl.ANY)],
        out_specs=row,
        scratch_shapes=[pltpu.VMEM((2, TOP_K, MOE_TILE, dm), F32), pltpu.SemaphoreType.DMA((2,))],
        compiler_params=pltpu.CompilerParams(
            dimension_semantics=("arbitrary",), vmem_limit_bytes=VMEM_LIMIT_BYTES),
        name="moe_combine",
    )(slot, slot, gate_cols, base, x, mods, ln_g[None, :], ln_b[None, :], ys)


def _ffn_body(x, wg_ref, wu_ref, wd_ref, row_lo, row_hi):
    x = x.astype(BF16)
    h1 = _dot(x, wg_ref[...].astype(BF16))
    h2 = _dot(x, wu_ref[...].astype(BF16))
    h = (h1 * jax.nn.sigmoid(h1)) * h2
    if row_lo is not None:
        row = lax.broadcasted_iota(jnp.int32, h.shape, 0)
        h = jnp.where((row >= row_lo) & (row < row_hi), h, 0.0)
    return _dot(h.astype(BF16), wd_ref[...].astype(BF16))


def _ffn_items_kernel(tile_ref, exp_ref, lo_ref, hi_ref, first_ref, n_ref, newexp_ref, nxt_ref, par_ref,
                      x_ref, wg_hbm, wu_hbm, wd_hbm, o_ref, wg_buf, wu_buf, wd_buf, wg_bf, wu_bf, wd_bf, sems,
                      *, layer):
    del tile_ref
    w = pl.program_id(0)

    def weight_copies(e, buf):
        return (pltpu.make_async_copy(wg_hbm.at[layer, e], wg_buf.at[buf], sems.at[buf]),
                pltpu.make_async_copy(wu_hbm.at[layer, e], wu_buf.at[buf], sems.at[buf]),
                pltpu.make_async_copy(wd_hbm.at[layer, e], wd_buf.at[buf], sems.at[buf]))

    @pl.when(w == 0)
    def _():
        for cp in weight_copies(exp_ref[0], 0):
            cp.start()

    @pl.when((w < n_ref[0]) & (newexp_ref[w] == 1))
    def _():
        for buf in range(2):
            @pl.when(par_ref[w] == buf)
            def _(buf=buf):
                for cp in weight_copies(exp_ref[w], buf):
                    cp.wait()

                @pl.when(nxt_ref[w] >= 0)
                def _():
                    for cp in weight_copies(nxt_ref[w], 1 - buf):
                        cp.start()

                wg_bf[...] = wg_buf[buf].astype(BF16)
                wu_bf[...] = wu_buf[buf].astype(BF16)
                wd_bf[...] = wd_buf[buf].astype(BF16)

    @pl.when(w < n_ref[0])
    def _():
        y = _ffn_body(x_ref[...], wg_bf, wu_bf, wd_bf, lo_ref[w], hi_ref[w])

        @pl.when(first_ref[w] == 1)
        def _():
            o_ref[...] = y

        @pl.when(first_ref[w] == 0)
        def _():
            o_ref[...] += y


def expert_ffn(xs, items, wg, wu, wd, layer):
    rows, dm = xs.shape
    de = wg.shape[-1]
    n_items = items[0].shape[0]
    n_prefetch = len(items)

    def row_map(w, tile, *_):
        return (tile[w], 0)

    hbm = pl.BlockSpec(memory_space=pl.ANY)
    return pl.pallas_call(
        functools.partial(_ffn_items_kernel, layer=layer),
        out_shape=jax.ShapeDtypeStruct((rows, dm), F32),
        grid_spec=pltpu.PrefetchScalarGridSpec(
            num_scalar_prefetch=n_prefetch,
            grid=(n_items,),
            in_specs=[pl.BlockSpec((MOE_TILE, dm), row_map), hbm, hbm, hbm],
            out_specs=pl.BlockSpec((MOE_TILE, dm), row_map),
            scratch_shapes=[pltpu.VMEM((2, dm, de), F32), pltpu.VMEM((2, dm, de), F32),
                            pltpu.VMEM((2, de, dm), F32), pltpu.VMEM((dm, de), BF16),
                            pltpu.VMEM((dm, de), BF16), pltpu.VMEM((de, dm), BF16),
                            pltpu.SemaphoreType.DMA((2,))]),
        compiler_params=pltpu.CompilerParams(
            dimension_semantics=("arbitrary",), vmem_limit_bytes=FFN_VMEM_LIMIT_BYTES),
        name="expert_ffn",
    )(*items, xs, wg, wu, wd)


def _ffn_dense_kernel(x_ref, wg_ref, wu_ref, wd_ref, o_ref):
    o_ref[...] = _ffn_body(x_ref[...], wg_ref, wu_ref, wd_ref, None, None)


def shared_ffn(x, wg, wu, wd):
    tt, dm = x.shape
    de = wg.shape[-1]
    row = pl.BlockSpec((MOE_TILE, dm), lambda i: (i, 0))
    return pl.pallas_call(
        _ffn_dense_kernel,
        out_shape=jax.ShapeDtypeStruct((tt, dm), F32),
        grid=(tt // MOE_TILE,),
        in_specs=[row, pl.BlockSpec((dm, de), lambda i: (0, 0)), pl.BlockSpec((dm, de), lambda i: (0, 0)),
                  pl.BlockSpec((de, dm), lambda i: (0, 0))],
        out_specs=row,
        compiler_params=pltpu.CompilerParams(
            dimension_semantics=("arbitrary",), vmem_limit_bytes=VMEM_LIMIT_BYTES),
        name="shared_ffn",
    )(x, wg, wu, wd)


def _ffn_items(counts, n_rows):
    n_tiles = n_rows // MOE_TILE
    n_items = n_tiles + N_EXPERTS - 1
    end = jnp.cumsum(counts)
    start = end - counts
    first_tile = start // MOE_TILE
    tiles_of = jnp.where(counts > 0, (end - 1) // MOE_TILE - first_tile + 1, 0)
    item_end = jnp.cumsum(tiles_of)
    item_start = item_end - tiles_of
    total = item_end[-1]
    w = jnp.minimum(jnp.arange(n_items, dtype=jnp.int32), total - 1)
    exp = jnp.sum((item_end[None, :] <= w[:, None]).astype(jnp.int32), axis=1)
    onehot = (exp[:, None] == jnp.arange(N_EXPERTS, dtype=jnp.int32)[None, :]).astype(jnp.int32)
    pick = lambda v: jnp.sum(onehot * v[None, :], axis=1)
    tile = pick(first_tile) + (w - pick(item_start))
    lo = jnp.maximum(pick(start) - tile * MOE_TILE, 0)
    hi = jnp.minimum(pick(end) - tile * MOE_TILE, MOE_TILE)
    prev_tile = jnp.concatenate([jnp.full((1,), -1, jnp.int32), tile[:-1]])
    first = (tile != prev_tile).astype(jnp.int32)
    prev_exp = jnp.concatenate([jnp.full((1,), -1, jnp.int32), exp[:-1].astype(jnp.int32)])
    newexp = (exp != prev_exp).astype(jnp.int32)
    parity = (jnp.cumsum(newexp) - 1) % 2
    ids = jnp.arange(N_EXPERTS, dtype=jnp.int32)
    later = (ids[None, :] > ids[:, None]) & (counts[None, :] > 0)
    next_of = jnp.min(jnp.where(later, ids[None, :], N_EXPERTS), axis=1)
    next_of = jnp.where(next_of == N_EXPERTS, -1, next_of)
    i32 = lambda v: v.astype(jnp.int32)
    return start.astype(jnp.int32), (i32(tile), i32(exp), i32(lo), i32(hi), first, i32(total).reshape(1),
                                     newexp, i32(pick(next_of)), i32(parity))


def moe_sublayer(x, t, mods, ln_g, ln_b, layout, layer, w_router, router_bias, w_gate, w_up, w_down,
                 ws_gate, ws_up, ws_down):
    tt, dm = t.shape
    eidx, pos, gate, counts = route(t, w_router, router_bias)
    start, items = _ffn_items(counts, tt * TOP_K)
    ids = jnp.arange(N_EXPERTS, dtype=jnp.int32)
    slot = pos + jnp.sum(jnp.where(eidx[..., None] == ids, start, 0), axis=-1)
    xs = dispatch(t, slot)
    ys = expert_ffn(xs, items, w_gate, w_up, w_down, layer)
    shared = shared_ffn(t, ws_gate, ws_up, ws_down)
    gate_cols = jnp.swapaxes(gate, 1, 2).reshape(tt, 8)
    return combine(shared, ys, gate_cols, slot, x, mods, ln_g, ln_b, layout)


TOKEN_TILE = 256
MOD_SHIFT1, MOD_SCALE1, MOD_GATE1, MOD_SHIFT2, MOD_SCALE2, MOD_GATE2 = range(6)


class Layout:
    def __init__(self, n_ctx_seq, ctx_len, n_lat_seq, lat_len):
        self.n_ctx_seq, self.ctx_len, self.n_lat_seq, self.lat_len = n_ctx_seq, ctx_len, n_lat_seq, lat_len
        self.n_ctx = n_ctx_seq * ctx_len
        self.n_tok = self.n_ctx + n_lat_seq * lat_len
        assert ctx_len % TOKEN_TILE == 0 and lat_len % TOKEN_TILE == 0

    def cond_row(self, tile):
        ctx_tiles = self.n_ctx // TOKEN_TILE
        return jnp.where(tile < ctx_tiles, 0, 1 + (tile - ctx_tiles) // (self.lat_len // TOKEN_TILE))

    def tile_pos(self, tile):
        ctx_tiles = self.n_ctx // TOKEN_TILE
        per_ctx, per_lat = self.ctx_len // TOKEN_TILE, self.lat_len // TOKEN_TILE
        is_ctx = tile < ctx_tiles
        return (jnp.where(is_ctx, tile % per_ctx, (tile - ctx_tiles) % per_lat),
                jnp.where(is_ctx, per_ctx, per_lat))


def _premix_kernel(x_ref, prev_ref, next_ref, mod_ref, mp_ref, mn_ref, *out_refs, layout):
    tm = x_ref.shape[0]
    pos, per_seq = layout.tile_pos(pl.program_id(0))
    shift, scale = mod_ref[MOD_SHIFT1:MOD_SHIFT1 + 1, :], mod_ref[MOD_SCALE1:MOD_SCALE1 + 1, :]
    mod = lambda t: t * (1.0 + scale) + shift
    h = mod(x_ref[...])
    h_before = jnp.where(pos > 0, mod(prev_ref[...]), 0.0)
    h_after = jnp.where(pos < per_seq - 1, mod(next_ref[...]), 0.0)
    row = lax.broadcasted_iota(jnp.int32, h.shape, 0)
    d_prev = jnp.where(row == 0, h_before, pltpu.roll(h, 1, 0)) - h
    d_next = jnp.where(row == tm - 1, h_after, pltpu.roll(h, tm - 1, 0)) - h
    for i, o_ref in enumerate(out_refs):
        o_ref[...] = (h + mp_ref[i:i + 1, :] * d_prev + mn_ref[i:i + 1, :] * d_next).astype(BF16)


def premix(x, mods, mix_prev, mix_next, layout):
    tt, dm = x.shape
    nt = tt // TOKEN_TILE
    zero = jnp.zeros((1, dm), F32)
    prev_rows = jnp.concatenate([zero, x[TOKEN_TILE - 1::TOKEN_TILE][:-1]], axis=0)[:, None, :]
    next_rows = jnp.concatenate([x[::TOKEN_TILE][1:], zero], axis=0)[:, None, :]
    tile = pl.BlockSpec((TOKEN_TILE, dm), lambda i: (i, 0))
    edge = pl.BlockSpec((None, 1, dm), lambda i: (i, 0, 0))
    par = pl.BlockSpec((6, dm), lambda i: (0, 0))
    return pl.pallas_call(
        functools.partial(_premix_kernel, layout=layout),
        out_shape=[jax.ShapeDtypeStruct((tt, dm), BF16)] * 6,
        grid=(nt,),
        in_specs=[tile, edge, edge, pl.BlockSpec((None, 6, dm), lambda i: (layout.cond_row(i), 0, 0)), par, par],
        out_specs=[tile] * 6,
        compiler_params=pltpu.CompilerParams(
            dimension_semantics=("arbitrary",), vmem_limit_bytes=VMEM_LIMIT_BYTES),
        name="premix",
    )(x, prev_rows, next_rows, mods, mix_prev, mix_next)


def _residual_ln(x, y, gate, g, b):
    z = ALPHA * x + gate * y
    mu = jnp.mean(z, axis=-1, keepdims=True)
    zc = z - mu
    var = jnp.mean(zc * zc, axis=-1, keepdims=True)
    return zc * lax.rsqrt(var + LN_EPS) * g + b


def _post_kernel(x_ref, y_ref, mod_ref, g_ref, b_ref, *out_refs, gate_row, mod_rows):
    xn = _residual_ln(x_ref[...], y_ref[...], mod_ref[gate_row:gate_row + 1, :], g_ref[...], b_ref[...])
    out_refs[0][...] = xn
    if mod_rows is not None:
        sh, sc = mod_rows
        out_refs[1][...] = xn * (1.0 + mod_ref[sc:sc + 1, :]) + mod_ref[sh:sh + 1, :]


def post(x, y, mods, ln_g, ln_b, layout, gate_row, mod_rows):
    tt, dm = x.shape
    tile = pl.BlockSpec((TOKEN_TILE, dm), lambda i: (i, 0))
    vec = pl.BlockSpec((1, dm), lambda i: (0, 0))
    n_out = 1 if mod_rows is None else 2
    return pl.pallas_call(
        functools.partial(_post_kernel, gate_row=gate_row, mod_rows=mod_rows),
        out_shape=[jax.ShapeDtypeStruct((tt, dm), F32)] * n_out,
        grid=(tt // TOKEN_TILE,),
        in_specs=[tile, tile, pl.BlockSpec((None, 6, dm), lambda i: (layout.cond_row(i), 0, 0)), vec, vec],
        out_specs=[tile] * n_out,
        compiler_params=pltpu.CompilerParams(
            dimension_semantics=("arbitrary",), vmem_limit_bytes=VMEM_LIMIT_BYTES),
        name="post",
    )(x, y, mods, ln_g[None, :], ln_b[None, :])


def _pool_kernel(x_ref, mod_ref, w_ref, scale_ref, o_ref, *, grid_w):
    seq_len = x_ref.shape[0]
    t_idx = lax.broadcasted_iota(jnp.int32, x_ref.shape, 0)

    def shifted(v, k, pos, extent, stride):
        if k > 0:
            return jnp.where(pos < extent - k, pltpu.roll(v, seq_len - k * stride, 0), 0.0)
        return jnp.where(pos >= -k, pltpu.roll(v, -k * stride, 0), 0.0)

    def box(v, w, pos, extent, stride):
        m = w // 2
        ahead, behind, step = v, v, 1
        while step < m:
            ahead = ahead + shifted(ahead, step, pos, extent, stride)
            behind = behind + shifted(behind, -step, pos, extent, stride)
            step *= 2
        total = ahead + shifted(behind, -1, pos, extent, stride)
        cnt = jnp.minimum(pos + m, extent) - jnp.maximum(pos - m, 0)
        return total, cnt.astype(F32)

    h = x_ref[...] * (1.0 + mod_ref[MOD_SCALE1:MOD_SCALE1 + 1, :]) + mod_ref[MOD_SHIFT1:MOD_SHIFT1 + 1, :]
    for gi, w in enumerate(POOL_WINDOWS):
        @pl.when(pl.program_id(1) == gi)
        def _(w=w):
            if grid_w is None:
                s, cnt = box(h, w, t_idx, seq_len, 1)
            else:
                s, cr = box(h, w, t_idx // grid_w, seq_len // grid_w, grid_w)
                s, cc = box(s, w, t_idx % grid_w, grid_w, 1)
                cnt = cr * cc
            p = (s / cnt - h).astype(BF16)
            o_ref[...] = _dot(p, w_ref[...].astype(BF16)) * scale_ref[...]


def pool_mix(x, mods, pool_w, pool_scale, layout, latent):
    dm = x.shape[1]
    n_grp = len(POOL_WINDOWS)
    if latent:
        n_seq, seq_len, first_blk, cond0 = layout.n_lat_seq, layout.lat_len, layout.n_ctx // layout.lat_len, 1
    else:
        n_seq, seq_len, first_blk, cond0 = layout.n_ctx_seq, layout.ctx_len, 0, None
    cond = (lambda i: 0) if cond0 is None else (lambda i: cond0 + i)
    return pl.pallas_call(
        functools.partial(_pool_kernel, grid_w=GRID_W if latent else None),
        out_shape=jax.ShapeDtypeStruct((n_seq * seq_len, dm), F32),
        grid=(n_seq, n_grp),
        in_specs=[pl.BlockSpec((seq_len, POOL_GROUP), lambda i, g: (first_blk + i, g)),
                  pl.BlockSpec((None, 6, POOL_GROUP), lambda i, g: (cond(i), 0, g)),
                  pl.BlockSpec((None, POOL_GROUP, POOL_GROUP), lambda i, g: (g, 0, 0)),
                  pl.BlockSpec((1, POOL_GROUP), lambda i, g: (0, g))],
        out_specs=pl.BlockSpec((seq_len, POOL_GROUP), lambda i, g: (i, g)),
        compiler_params=pltpu.CompilerParams(
            dimension_semantics=("arbitrary", "arbitrary"), vmem_limit_bytes=VMEM_LIMIT_BYTES),
        name="pool_mix",
    )(x, mods, pool_w, pool_scale[None, :])


def kernel(x_prompt, x_sample, c, state_rwkv, c_ctx, w_ada, b_ada, ln_g, ln_b, rw_mix_prev, rw_mix_next, rw_w_r, rw_w_k, rw_w_v, rw_w_o, rw_w0, rw_w1, rw_w2, rw_a0, rw_a1, rw_a2, rw_g1, rw_g2, rw_k_k, rw_k_a, rw_r_k, rw_gn_g, rw_gn_b, pool_w, pool_scale, moe_router, moe_router_bias, moe_w_gate, moe_w_up, moe_w_down, moe_ws_gate, moe_ws_up, moe_ws_down):
    dm = D_MODEL
    H, N = RWKV_HEADS, RWKV_HEAD
    bc, lc_, _ = x_prompt.shape
    bl, ll, _ = x_sample.shape
    layout = Layout(bc, lc_, bl, ll)
    tc = layout.n_ctx
    x = jnp.concatenate([x_prompt.reshape(tc, dm), x_sample.reshape(bl * ll, dm)], axis=0)
    cond = jnp.concatenate([c_ctx[None, :], c], axis=0)
    n_cond = cond.shape[0]
    assert n_cond <= 8
    cond_pad = jnp.zeros((8, dm), F32).at[:n_cond].set(jax.nn.silu(cond))
    new_states = []

    for l in range(DEPTH):
        mods = (matmul(cond_pad, w_ada, tn=1024, layer=l) + b_ada[l]).reshape(8, 6, dm)
        j = l // 2
        if l % 2 == 0:
            xr, xw, xk, xv, xa, xg = premix(x, mods, rw_mix_prev[j], rw_mix_next[j], layout)
            r = matmul(xr, rw_w_r[j])
            k = matmul(xk, rw_w_k[j])
            v = matmul(xv, rw_w_v[j])
            g = lora(xg, rw_g1[j], rw_g2[j], jax.nn.sigmoid)
            wl = [lora(xw, rw_w1[j, d], rw_w2[j, d], jnp.tanh) for d in range(2)]
            al = [lora(xa, rw_a1[j, d], rw_a2[j, d]) for d in range(2)]
            par = jnp.zeros((16, dm), F32).at[:9].set(jnp.stack(
                [rw_w0[j, 0], rw_w0[j, 1], rw_a0[j, 0], rw_a0[j, 1], rw_k_k[j], rw_k_a[j],
                 rw_r_k[j].reshape(dm), rw_gn_g[j], rw_gn_b[j]]))
            out_c, s_fin = rwkv_core(r, k, v, g, wl, al, par, jnp.zeros((bc, 2, H, N, N), F32), 0, lc_)
            out_l, _ = rwkv_core(r, k, v, g, wl, al, par, state_rwkv[:, j], tc, ll)
            new_states.append(s_fin)
            mix = matmul(jnp.concatenate([out_c, out_l], axis=0), rw_w_o[j])
        else:
            mix = jnp.concatenate([pool_mix(x, mods, pool_w[j], pool_scale[j], layout, False),
                                   pool_mix(x, mods, pool_w[j], pool_scale[j], layout, True)], axis=0)
        x, h2 = post(x, mix, mods, ln_g[l, 0], ln_b[l, 0], layout, MOD_GATE1, (MOD_SHIFT2, MOD_SCALE2))
        x = moe_sublayer(x, h2, mods, ln_g[l, 1], ln_b[l, 1], layout, l, moe_router[l], moe_router_bias[l],
                         moe_w_gate, moe_w_up, moe_w_down, moe_ws_gate[l], moe_ws_up[l], moe_ws_down[l])

    return (x[:tc].reshape(bc, lc_, dm), x[tc:].reshape(bl, ll, dm), jnp.stack(new_states, axis=1))
```

```python
import functools

import jax
import jax.numpy as jnp
from jax import lax
from jax.experimental import pallas as pl
from jax.experimental.pallas import tpu as pltpu

D_MODEL = 2048
DEPTH = 2
GRID_W = 64
RWKV_HEAD = 64
RWKV_HEADS = D_MODEL // RWKV_HEAD
POOL_WINDOWS = (2, 4, 8, 16)
POOL_GROUP = D_MODEL // len(POOL_WINDOWS)
N_EXPERTS = 64
TOP_K = 6
N_GROUPS = 8
TOPK_GROUPS = 4
ROUTED_SCALE = 2.5
ALPHA = (2 * DEPTH) ** 0.25
LN_EPS = 1e-5
GN_EPS = 64e-5

F32 = jnp.float32
BF16 = jnp.bfloat16

VMEM_LIMIT_BYTES = 48 * 1024 * 1024
FFN_VMEM_LIMIT_BYTES = 56 * 1024 * 1024

SCAN_CHUNK = 64
SCAN_HEADS = 4
SCAN_LANES = SCAN_HEADS * RWKV_HEAD
SCAN_BLOCK_ROWS = 1024
SCAN_PREP_CHUNKS = 4
MOE_TILE = 256


def _dot(a, b):
    return jnp.dot(a, b, preferred_element_type=F32)


def _dot_nt(a, b):
    return lax.dot_general(a, b, (((1,), (1,)), ((), ())), preferred_element_type=F32)


def _dot_tn(a, b):
    return lax.dot_general(a, b, (((0,), (0,)), ((), ())), preferred_element_type=F32)


def _mm_kernel(x_ref, w_ref, o_ref):
    o_ref[...] = _dot(x_ref[...].astype(BF16), w_ref[...].astype(BF16))


def matmul(x, w, *, tm=1024, tn=512, layer=None):
    m, k = x.shape
    n = w.shape[-1]
    tm = min(tm, m)
    tn = min(tn, n)
    assert m % tm == 0 and n % tn == 0
    if layer is None:
        w_spec = pl.BlockSpec((k, tn), lambda j, i: (0, j))
    else:
        w_spec = pl.BlockSpec((None, k, tn), lambda j, i: (layer, 0, j))
    return pl.pallas_call(
        _mm_kernel,
        out_shape=jax.ShapeDtypeStruct((m, n), F32),
        grid=(n // tn, m // tm),
        in_specs=[pl.BlockSpec((tm, k), lambda j, i: (i, 0)), w_spec],
        out_specs=pl.BlockSpec((tm, tn), lambda j, i: (i, j)),
        compiler_params=pltpu.CompilerParams(
            dimension_semantics=("arbitrary", "arbitrary"),
            vmem_limit_bytes=VMEM_LIMIT_BYTES),
        name="matmul",
    )(x, w)


def _lora_kernel(x_ref, w1_ref, w2_ref, o_ref, *, act):
    h = _dot(x_ref[...].astype(BF16), w1_ref[...].astype(BF16))
    if act is not None:
        h = act(h)
    o_ref[...] = _dot(h.astype(BF16), w2_ref[...].astype(BF16))


def lora(x, w1, w2, act=None, *, tm=1024):
    m, k = x.shape
    r = w1.shape[1]
    n = w2.shape[1]
    assert m % tm == 0
    return pl.pallas_call(
        functools.partial(_lora_kernel, act=act),
        out_shape=jax.ShapeDtypeStruct((m, n), F32),
        grid=(m // tm,),
        in_specs=[pl.BlockSpec((tm, k), lambda i: (i, 0)), pl.BlockSpec((k, r), lambda i: (0, 0)),
                  pl.BlockSpec((r, n), lambda i: (0, 0))],
        out_specs=pl.BlockSpec((tm, n), lambda i: (i, 0)),
        compiler_params=pltpu.CompilerParams(
            dimension_semantics=("arbitrary",), vmem_limit_bytes=VMEM_LIMIT_BYTES),
        name="lora",
    )(x, w1, w2)


def _split3(x):
    hi = x.astype(BF16)
    r1 = x - hi.astype(F32)
    mid = r1.astype(BF16)
    lo = (r1 - mid.astype(F32)).astype(BF16)
    return hi, mid, lo


def _softplus(z):
    return jnp.maximum(z, 0.0) + jnp.log(1.0 + jnp.exp(-jnp.abs(z)))


def _scan_kernel(r_ref, k_ref, v_ref, g_ref, wl0_ref, wl1_ref, al0_ref, al1_ref, par_ref, s0_ref,
                 out_ref, sfin_ref, s_scr, kk_ref, lw_ref, a_ref, kd_ref, bonus_ref, y_ref,
                 pr_s, ak_s, lra_s, uy_s, gend_s, lvl_s, *, nc, nseq, prep):
    C, N, HB, W = SCAN_CHUNK, RWKV_HEAD, SCAN_HEADS, SCAN_LANES
    t_idx = lax.broadcasted_iota(jnp.int32, (C, W), 0)
    s_idx = lax.broadcasted_iota(jnp.int32, (C, W), 1) % C
    eye_row = (s_idx == t_idx).astype(F32)
    tri_t = lax.broadcasted_iota(jnp.int32, (C, C), 0)
    tri_s = lax.broadcasted_iota(jnp.int32, (C, C), 1)
    blk_r = lax.broadcasted_iota(jnp.int32, (W, W), 0) // N
    blk_c = lax.broadcasted_iota(jnp.int32, (W, W), 1) // N
    bd_mask = blk_r == blk_c
    bd_mask_bf = bd_mask.astype(BF16)

    def expand(x_row_bf):
        return jnp.concatenate([x_row_bf] * HB, axis=0) * bd_mask_bf

    n_chunks = nseq * nc
    for s in range(nseq):
        for d in range(2):
            s_scr[2 * s + d] = jnp.zeros((W, W), F32)
            for h in range(HB):
                s_scr[2 * s + d, h * N:(h + 1) * N, h * N:(h + 1) * N] = s0_ref[s, d, h]
    y_ref[...] = jnp.zeros_like(y_ref)

    levels = []
    b = 2
    while b < C:
        levels.append(b)
        b *= 2

    @pl.when((pl.program_id(0) == 0) & (pl.program_id(1) == 0))
    def _():
        t_bd = lax.broadcasted_iota(jnp.int32, (W, W), 0) % C
        s_bd = lax.broadcasted_iota(jnp.int32, (W, W), 1) % C
        for li, b in enumerate(levels):
            tb, sb = t_bd // b, s_bd // b
            lvl_s[0, li] = (bd_mask & (tb % 2 == 1) & (sb == tb - 1)).astype(BF16)
            lvl_s[1, li] = (bd_mask & (tb % 2 == 0) & (sb == tb + 1)).astype(BF16)

    ones_bd = bd_mask_bf
    RB = 4 * C
    par = lambda i: par_ref[i:i + 1, :]

    def head_sum(x):
        hi, lo = _split2(x)
        return _dot(hi, ones_bd) + _dot(lo, ones_bd)

    def prologue(i, carry):
        rows = pl.ds(pl.multiple_of(i * RB, RB), RB)
        k = k_ref[rows, :]
        kkp = k * par(4)
        kk_ref[rows, :] = kkp * lax.rsqrt(jnp.maximum(head_sum(kkp * kkp), 1e-24))
        kd_sum = jnp.zeros_like(k)
        for d, (wl_ref, al_ref) in enumerate(((wl0_ref, al0_ref), (wl1_ref, al1_ref))):
            w_log = -_softplus(-(par(d) + wl_ref[rows, :])) - 0.5
            lw_ref[d, rows, :] = -jnp.exp(w_log)
            a = jax.nn.sigmoid(par(2 + d) + al_ref[rows, :])
            a_ref[d, rows, :] = a
            kd = k * (1.0 + (a - 1.0) * par(5))
            kd_ref[d, rows, :] = kd
            kd_sum = kd_sum + kd
        bonus_ref[rows, :] = head_sum(r_ref[rows, :] * kd_sum * par(6)) * v_ref[rows, :]
        return carry

    lax.fori_loop(0, nseq * nc * C // RB, prologue, 0)

    def interleave(chains):
        live = list(chains)
        while live:
            nxt = []
            for ch in live:
                try:
                    next(ch)
                    nxt.append(ch)
                except StopIteration:
                    pass
            live = nxt

    def prepare_chunk(d, ci):
        fwd = d == 0
        strict = (s_idx < t_idx) if fwd else (s_idx > t_idx)
        incl = (s_idx <= t_idx) if fwd else (s_idx >= t_idx)
        tri = ((tri_s <= tri_t) if fwd else (tri_s >= tri_t)).astype(BF16)
        rows = pl.ds(pl.multiple_of(ci * C, C), C)
        lw = lw_ref[d, rows, :]
        hi, mid, lo = _split3(lw)
        lc = _dot(tri, hi) + _dot(tri, mid) + _dot(tri, lo)
        yield
        kk = kk_ref[rows, :]
        g_in = jnp.exp(lc)
        g_inv = jnp.exp(-lc)
        g_ex = jnp.exp(lc - lw)
        last = C - 1 if fwd else 0
        gend_s[d, ci] = jnp.broadcast_to(g_in[last:last + 1, :], (8, W))
        bt = (kk * g_ex).astype(BF16)
        rt = (r_ref[rows, :] * g_in).astype(BF16)
        at = (-(kk * a_ref[d, rows, :]) * g_inv).astype(BF16)
        kt = (kd_ref[d, rows, :] * g_inv).astype(BF16)
        br = jnp.concatenate([bt, rt], axis=0)
        ak_s[d, ci] = jnp.concatenate([at, kt], axis=0)
        la = _dot_nt(br, expand(at))
        lk = _dot_nt(br, expand(kt))
        yield
        lba = la[:C]
        lra_s[d, ci] = jnp.where(incl, la[C:], 0.0).astype(BF16)
        lbrk = jnp.concatenate([jnp.where(strict, lk[:C], 0.0), jnp.where(incl, lk[C:], 0.0)], axis=0)
        wy2 = _dot(lbrk.astype(BF16), expand(v_ref[rows, :].astype(BF16)))

        pair = (s_idx == t_idx - 1) if fwd else (s_idx == t_idx + 1)
        inv = eye_row + jnp.where(pair & (t_idx % 2 == (1 if fwd else 0)), lba, 0.0)
        lba4 = jnp.concatenate([lba.astype(BF16)] * HB, axis=0)
        for li in range(len(levels)):
            m1 = _dot(inv.astype(BF16), lba4 * lvl_s[d, li])
            yield
            inv = inv + _dot(m1.astype(BF16), expand(inv.astype(BF16)))
            yield
        inv_bf = inv.astype(BF16)
        pu = _dot(inv_bf, expand(bt))
        u0 = _dot(inv_bf, expand(wy2[:C].astype(BF16)))
        yield
        pr_s[d, ci] = jnp.concatenate([pu.astype(BF16), rt], axis=0)
        uy_s[d, ci] = jnp.concatenate([u0, wy2[C:]], axis=0)

    def prepare(i, carry):
        interleave([prepare_chunk(d, i * prep + j) for j in range(prep) for d in range(2)])
        return carry

    lax.fori_loop(0, n_chunks // prep, prepare, 0)

    def advance_chunk(s, d, ci):
        rows = pl.ds(pl.multiple_of(ci * C, C), C)
        s_bd = s_scr[2 * s + d]
        py = _dot_nt(pr_s[d, ci], s_bd.astype(BF16))
        yield
        uy = uy_s[d, ci]
        u_bf = (py[:C] + uy[:C]).astype(BF16)
        upd = _dot_tn(jnp.concatenate([u_bf, v_ref[rows, :].astype(BF16)], axis=0), ak_s[d, ci])
        y = py[C:] + uy[C:] + _dot(lra_s[d, ci], expand(u_bf))
        yield
        y_ref[rows, :] += y
        s_scr[2 * s + d] = jnp.where(bd_mask, (s_bd + upd) * gend_s[d, ci][0:1, :], 0.0)

    def advance(c, carry):
        interleave([advance_chunk(s, d, s * nc + (c if d == 0 else nc - 1 - c))
                    for s in range(nseq) for d in range(2)])
        return carry

    lax.fori_loop(0, nc, advance, 0)

    for s in range(nseq):
        for d in range(2):
            for h in range(HB):
                sfin_ref[s, d, h] = s_scr[2 * s + d, h * N:(h + 1) * N, h * N:(h + 1) * N]

    def epilogue(i, carry):
        rows = pl.ds(pl.multiple_of(i * RB, RB), RB)
        y = y_ref[rows, :]
        yc = y - head_sum(y) * (1.0 / N)
        var = head_sum(yc * yc) * (1.0 / N)
        yn = yc * lax.rsqrt(var + GN_EPS) * par(7) + par(8)
        out_ref[rows, :] = ((yn + bonus_ref[rows, :]) * g_ref[rows, :]).astype(BF16)
        return carry

    lax.fori_loop(0, nseq * nc * C // RB, epilogue, 0)


def rwkv_core(r, k, v, g, wl, al, par, s0, first_row, seq_len):
    b = s0.shape[0]
    d = r.shape[1]
    nseq = max(1, min(b, SCAN_BLOCK_ROWS // seq_len))
    l = nseq * seq_len
    assert first_row % l == 0 and b % nseq == 0
    nc = seq_len // SCAN_CHUNK
    first_blk = first_row // l
    seq = pl.BlockSpec((l, SCAN_LANES), lambda i, j: (first_blk + i, j))
    oseq = pl.BlockSpec((l, SCAN_LANES), lambda i, j: (i, j))
    st = pl.BlockSpec((nseq, 2, SCAN_HEADS, RWKV_HEAD, RWKV_HEAD), lambda i, j: (i, 0, j, 0, 0))
    seq_scr = pltpu.VMEM((l, SCAN_LANES), F32)
    seq2_scr = pltpu.VMEM((2, l, SCAN_LANES), F32)
    C, W = SCAN_CHUNK, SCAN_LANES
    per_chunk = lambda rows, dt: pltpu.VMEM((2, nseq * nc, rows, W), dt)
    return pl.pallas_call(
        functools.partial(_scan_kernel, nc=nc, nseq=nseq, prep=SCAN_PREP_CHUNKS),
        out_shape=(jax.ShapeDtypeStruct((b * seq_len, d), BF16), jax.ShapeDtypeStruct(s0.shape, F32)),
        grid=(b // nseq, d // SCAN_LANES),
        in_specs=[seq] * 8 + [pl.BlockSpec((16, SCAN_LANES), lambda i, j: (0, j)), st],
        out_specs=(oseq, st),
        scratch_shapes=[pltpu.VMEM((2 * nseq, SCAN_LANES, SCAN_LANES), F32),
                        seq_scr, seq2_scr, seq2_scr, seq2_scr, seq_scr, seq_scr,
                        per_chunk(2 * C, BF16), per_chunk(2 * C, BF16), per_chunk(C, BF16),
                        per_chunk(2 * C, F32), per_chunk(8, F32),
                        pltpu.VMEM((2, 5, W, W), BF16)],
        compiler_params=pltpu.CompilerParams(
            dimension_semantics=("arbitrary", "arbitrary"),
            vmem_limit_bytes=VMEM_LIMIT_BYTES),
        name="wkv_scan",
    )(r, k, v, g, wl[0], wl[1], al[0], al[1], par, s0)


def _split2(x):
    hi = x.astype(BF16)
    return hi, (x - hi.astype(F32)).astype(BF16)


def _first_index(hit_value, iota, sentinel, axis):
    return jnp.min(jnp.where(hit_value, iota, sentinel), axis=axis, keepdims=True)


def _router_kernel(h_ref, wrt_ref, bias_ref, eidx_ref, pos_ref, gate_ref, cnt_ref, cnt_scr):
    tm = h_ref.shape[0]
    E, G, GS = N_EXPERTS, N_GROUPS, N_EXPERTS // N_GROUPS
    neg = -jnp.inf

    @pl.when(pl.program_id(0) == 0)
    def _():
        cnt_scr[...] = jnp.zeros_like(cnt_scr)

    x_hi, x_lo = _split2(h_ref[...])
    w_hi, w_lo = _split2(wrt_ref[...])
    logits = _dot_nt(w_hi, x_hi) + (_dot_nt(w_hi, x_lo) + _dot_nt(w_lo, x_hi))
    scores = jax.nn.sigmoid(logits)
    biased = scores + bias_ref[...]

    g3 = biased.reshape(G, GS, tm)
    w_iota = lax.broadcasted_iota(jnp.int32, (G, GS, tm), 1).astype(F32)
    m1 = jnp.max(g3, axis=1, keepdims=True)
    first = _first_index(g3 == m1, w_iota, float(GS), 1)
    m2 = jnp.max(jnp.where(w_iota == first, neg, g3), axis=1, keepdims=True)
    gs = (m1 + m2).reshape(G, tm)

    g_iota = lax.broadcasted_iota(jnp.int32, (G, tm), 0).astype(F32)
    gsel = jnp.zeros((G, tm), jnp.bool_)
    cur = gs
    for _ in range(TOPK_GROUPS):
        m = jnp.max(cur, axis=0, keepdims=True)
        hit = g_iota == _first_index(cur == m, g_iota, float(G), 0)
        gsel = gsel | hit
        cur = jnp.where(hit, neg, cur)

    masked = jnp.where(gsel[:, None, :], g3, neg).reshape(E, tm)
    e_iota = lax.broadcasted_iota(jnp.int32, (E, tm), 0).astype(F32)
    sel = jnp.zeros((E, tm), jnp.bool_)
    hits, ids = [], []
    cur = masked
    for _ in range(TOP_K):
        m = jnp.max(cur, axis=0, keepdims=True)
        f = _first_index(cur == m, e_iota, float(E), 0)
        hit = e_iota == f
        hits.append(hit)
        ids.append(f)
        sel = sel | hit
        cur = jnp.where(hit, neg, cur)

    ssum = jnp.sum(jnp.where(sel, scores, 0.0), axis=0, keepdims=True)
    gates = scores / (ssum + 1e-20) * ROUTED_SCALE

    before = (lax.broadcasted_iota(jnp.int32, (tm, tm), 0) < lax.broadcasted_iota(jnp.int32, (tm, tm), 1))
    sel_f = sel.astype(F32)
    rank = _dot(sel_f.astype(BF16), before.astype(BF16)) + cnt_scr[:, 0:1]
    cnt_new = cnt_scr[...] + jnp.sum(sel_f, axis=1, keepdims=True)
    cnt_scr[...] = cnt_new
    cnt_ref[...] = cnt_new.astype(jnp.int32)

    k_iota = lax.broadcasted_iota(jnp.int32, (8, tm), 0)
    eidx8 = jnp.zeros((8, tm), F32)
    pos8 = jnp.zeros((8, tm), F32)
    gate8 = jnp.zeros((8, tm), F32)
    for k in range(TOP_K):
        row = k_iota == k
        eidx8 = jnp.where(row, ids[k], eidx8)
        pos8 = jnp.where(row, jnp.sum(jnp.where(hits[k], rank, 0.0), axis=0, keepdims=True), pos8)
        gate8 = jnp.where(row, jnp.sum(jnp.where(hits[k], gates, 0.0), axis=0, keepdims=True), gate8)
    eidx_ref[...] = eidx8.astype(jnp.int32)
    pos_ref[...] = pos8.astype(jnp.int32)
    gate_ref[...] = gate8


def route(h, w_router, router_bias):
    tt, dm = h.shape
    nt = tt // MOE_TILE
    tok = pl.BlockSpec((None, 8, MOE_TILE), lambda i: (i, 0, 0))
    eidx, pos, gate, cnt = pl.pallas_call(
        _router_kernel,
        out_shape=(jax.ShapeDtypeStruct((nt, 8, MOE_TILE), jnp.int32),
                   jax.ShapeDtypeStruct((nt, 8, MOE_TILE), jnp.int32),
                   jax.ShapeDtypeStruct((nt, 8, MOE_TILE), F32),
                   jax.ShapeDtypeStruct((N_EXPERTS, 128), jnp.int32)),
        grid=(nt,),
        in_specs=[pl.BlockSpec((MOE_TILE, dm), lambda i: (i, 0)),
                  pl.BlockSpec((N_EXPERTS, dm), lambda i: (0, 0)),
                  pl.BlockSpec((N_EXPERTS, 1), lambda i: (0, 0))],
        out_specs=(tok, tok, tok, pl.BlockSpec((N_EXPERTS, 128), lambda i: (0, 0))),
        scratch_shapes=[pltpu.VMEM((N_EXPERTS, 128), F32)],
        compiler_params=pltpu.CompilerParams(
            dimension_semantics=("arbitrary",), vmem_limit_bytes=VMEM_LIMIT_BYTES),
        name="moe_router",
    )(h, w_router.T, router_bias[:, None])
    return eidx, pos, gate, cnt[:, 0]


def _dispatch_kernel(slot_ref, x_ref, xs_ref, sem):
    tm = x_ref.shape[0]

    def issue(t, carry):
        for k in range(TOP_K):
            pltpu.make_async_copy(x_ref.at[pl.ds(t, 1), :], xs_ref.at[pl.ds(slot_ref[k, t], 1), :],
                                  sem).start(priority=k % 2)
        return carry

    lax.fori_loop(0, tm, issue, 0)
    for k in range(TOP_K):
        pltpu.make_async_copy(x_ref, xs_ref.at[pl.ds(0, tm), :], sem).wait()


def dispatch(h, slot):
    tt, dm = h.shape
    nt = tt // MOE_TILE
    return pl.pallas_call(
        _dispatch_kernel,
        out_shape=jax.ShapeDtypeStruct((tt * TOP_K, dm), F32),
        grid=(nt,),
        in_specs=[pl.BlockSpec((None, 8, MOE_TILE), lambda i: (i, 0, 0), memory_space=pltpu.SMEM),
                  pl.BlockSpec((MOE_TILE, dm), lambda i: (i, 0))],
        out_specs=pl.BlockSpec(memory_space=pl.ANY),
        scratch_shapes=[pltpu.SemaphoreType.DMA],
        compiler_params=pltpu.CompilerParams(
            dimension_semantics=("arbitrary",), vmem_limit_bytes=VMEM_LIMIT_BYTES),
        name="moe_dispatch",
    )(slot, h)


def _combine_kernel(slot_ref, slot_nxt_ref, gate_ref, base_ref, x_ref,
                    mod_ref, g_ref, b_ref, ys_ref, o_ref, buf, sems):
    tm = base_ref.shape[0]
    i = pl.program_id(0)
    n = pl.num_programs(0)

    def gather(s_ref, slot_buf):
        def issue(t, carry):
            for k in range(TOP_K):
                pltpu.make_async_copy(ys_ref.at[pl.ds(s_ref[k, t], 1), :], buf.at[slot_buf, k, pl.ds(t, 1), :],
                                      sems.at[slot_buf]).start(priority=k % 2)
            return carry
        lax.fori_loop(0, tm, issue, 0)

    cur = i % 2
    for slot_buf in range(2):
        @pl.when((i == 0) & (cur == slot_buf))
        def _(slot_buf=slot_buf):
            gather(slot_ref, slot_buf)

        @pl.when((i + 1 < n) & (cur != slot_buf))
        def _(slot_buf=slot_buf):
            gather(slot_nxt_ref, slot_buf)

    for slot_buf in range(2):
        @pl.when(cur == slot_buf)
        def _(slot_buf=slot_buf):
            for k in range(TOP_K):
                pltpu.make_async_copy(ys_ref.at[pl.ds(0, tm), :], buf.at[slot_buf, k], sems.at[slot_buf]).wait()
            acc = base_ref[...]
            for k in range(TOP_K):
                acc = acc + gate_ref[:, k:k + 1] * buf[slot_buf, k]
            o_ref[...] = _residual_ln(x_ref[...], acc, mod_ref[MOD_GATE2:MOD_GATE2 + 1, :], g_ref[...], b_ref[...])


def combine(base, ys, gate_cols, slot, x, mods, ln_g, ln_b, layout):
    tt, dm = base.shape
    nt = tt // MOE_TILE
    smem_tok = pl.BlockSpec((None, 8, MOE_TILE), lambda i: (i, 0, 0), memory_space=pltpu.SMEM)
    smem_nxt = pl.BlockSpec((None, 8, MOE_TILE), lambda i: (jnp.minimum(i + 1, nt - 1), 0, 0),
                            memory_space=pltpu.SMEM)
    row = pl.BlockSpec((MOE_TILE, dm), lambda i: (i, 0))
    vec = pl.BlockSpec((1, dm), lambda i: (0, 0))
    return pl.pallas_call(
        _combine_kernel,
        out_shape=jax.ShapeDtypeStruct((tt, dm), F32),
        grid=(nt,),
        in_specs=[smem_tok, smem_nxt, pl.BlockSpec((MOE_TILE, 8), lambda i: (i, 0)), row, row,
                  pl.BlockSpec((None, 6, dm), lambda i: (layout.cond_row(i), 0, 0)), vec, vec,
                  pl.BlockSpec(memory_space=pl.ANY)],
        out_specs=row,
        scratch_shapes=[pltpu.VMEM((2, TOP_K, MOE_TILE, dm), F32), pltpu.SemaphoreType.DMA((2,))],
        compiler_params=pltpu.CompilerParams(
            dimension_semantics=("arbitrary",), vmem_limit_bytes=VMEM_LIMIT_BYTES),
        name="moe_combine",
    )(slot, slot, gate_cols, base, x, mods, ln_g[None, :], ln_b[None, :], ys)


def _ffn_body(x, wg_ref, wu_ref, wd_ref, row_lo, row_hi):
    x = x.astype(BF16)
    h1 = _dot(x, wg_ref[...].astype(BF16))
    h2 = _dot(x, wu_ref[...].astype(BF16))
    h = (h1 * jax.nn.sigmoid(h1)) * h2
    if row_lo is not None:
        row = lax.broadcasted_iota(jnp.int32, h.shape, 0)
        h = jnp.where((row >= row_lo) & (row < row_hi), h, 0.0)
    return _dot(h.astype(BF16), wd_ref[...].astype(BF16))


def _ffn_items_kernel(tile_ref, exp_ref, lo_ref, hi_ref, first_ref, n_ref, newexp_ref, nxt_ref, par_ref,
                      x_ref, wg_hbm, wu_hbm, wd_hbm, o_ref, wg_buf, wu_buf, wd_buf, wg_bf, wu_bf, wd_bf, sems,
                      *, layer):
    del tile_ref
    w = pl.program_id(0)

    def weight_copies(e, buf):
        return (pltpu.make_async_copy(wg_hbm.at[layer, e], wg_buf.at[buf], sems.at[buf]),
                pltpu.make_async_copy(wu_hbm.at[layer, e], wu_buf.at[buf], sems.at[buf]),
                pltpu.make_async_copy(wd_hbm.at[layer, e], wd_buf.at[buf], sems.at[buf]))

    @pl.when(w == 0)
    def _():
        for cp in weight_copies(exp_ref[0], 0):
            cp.start()

    @pl.when((w < n_ref[0]) & (newexp_ref[w] == 1))
    def _():
        for buf in range(2):
            @pl.when(par_ref[w] == buf)
            def _(buf=buf):
                for cp in weight_copies(exp_ref[w], buf):
                    cp.wait()

                @pl.when(nxt_ref[w] >= 0)
                def _():
                    for cp in weight_copies(nxt_ref[w], 1 - buf):
                        cp.start()

                wg_bf[...] = wg_buf[buf].astype(BF16)
                wu_bf[...] = wu_buf[buf].astype(BF16)
                wd_bf[...] = wd_buf[buf].astype(BF16)

    @pl.when(w < n_ref[0])
    def _():
        y = _ffn_body(x_ref[...], wg_bf, wu_bf, wd_bf, lo_ref[w], hi_ref[w])

        @pl.when(first_ref[w] == 1)
        def _():
            o_ref[...] = y

        @pl.when(first_ref[w] == 0)
        def _():
            o_ref[...] += y


def expert_ffn(xs, items, wg, wu, wd, layer):
    rows, dm = xs.shape
    de = wg.shape[-1]
    n_items = items[0].shape[0]
    n_prefetch = len(items)

    def row_map(w, tile, *_):
        return (tile[w], 0)

    hbm = pl.BlockSpec(memory_space=pl.ANY)
    return pl.pallas_call(
        functools.partial(_ffn_items_kernel, layer=layer),
        out_shape=jax.ShapeDtypeStruct((rows, dm), F32),
        grid_spec=pltpu.PrefetchScalarGridSpec(
            num_scalar_prefetch=n_prefetch,
            grid=(n_items,),
            in_specs=[pl.BlockSpec((MOE_TILE, dm), row_map), hbm, hbm, hbm],
            out_specs=pl.BlockSpec((MOE_TILE, dm), row_map),
            scratch_shapes=[pltpu.VMEM((2, dm, de), F32), pltpu.VMEM((2, dm, de), F32),
                            pltpu.VMEM((2, de, dm), F32), pltpu.VMEM((dm, de), BF16),
                            pltpu.VMEM((dm, de), BF16), pltpu.VMEM((de, dm), BF16),
                            pltpu.SemaphoreType.DMA((2,))]),
        compiler_params=pltpu.CompilerParams(
            dimension_semantics=("arbitrary",), vmem_limit_bytes=FFN_VMEM_LIMIT_BYTES),
        name="expert_ffn",
    )(*items, xs, wg, wu, wd)


def _ffn_dense_kernel(x_ref, wg_ref, wu_ref, wd_ref, o_ref):
    o_ref[...] = _ffn_body(x_ref[...], wg_ref, wu_ref, wd_ref, None, None)


def shared_ffn(x, wg, wu, wd):
    tt, dm = x.shape
    de = wg.shape[-1]
    row = pl.BlockSpec((MOE_TILE, dm), lambda i: (i, 0))
    return pl.pallas_call(
        _ffn_dense_kernel,
        out_shape=jax.ShapeDtypeStruct((tt, dm), F32),
        grid=(tt // MOE_TILE,),
        in_specs=[row, pl.BlockSpec((dm, de), lambda i: (0, 0)), pl.BlockSpec((dm, de), lambda i: (0, 0)),
                  pl.BlockSpec((de, dm), lambda i: (0, 0))],
        out_specs=row,
        compiler_params=pltpu.CompilerParams(
            dimension_semantics=("arbitrary",), vmem_limit_bytes=VMEM_LIMIT_BYTES),
        name="shared_ffn",
    )(x, wg, wu, wd)


def _ffn_items(counts, n_rows):
    n_tiles = n_rows // MOE_TILE
    n_items = n_tiles + N_EXPERTS - 1
    end = jnp.cumsum(counts)
    start = end - counts
    first_tile = start // MOE_TILE
    tiles_of = jnp.where(counts > 0, (end - 1) // MOE_TILE - first_tile + 1, 0)
    item_end = jnp.cumsum(tiles_of)
    item_start = item_end - tiles_of
    total = item_end[-1]
    w = jnp.minimum(jnp.arange(n_items, dtype=jnp.int32), total - 1)
    exp = jnp.sum((item_end[None, :] <= w[:, None]).astype(jnp.int32), axis=1)
    onehot = (exp[:, None] == jnp.arange(N_EXPERTS, dtype=jnp.int32)[None, :]).astype(jnp.int32)
    pick = lambda v: jnp.sum(onehot * v[None, :], axis=1)
    tile = pick(first_tile) + (w - pick(item_start))
    lo = jnp.maximum(pick(start) - tile * MOE_TILE, 0)
    hi = jnp.minimum(pick(end) - tile * MOE_TILE, MOE_TILE)
    prev_tile = jnp.concatenate([jnp.full((1,), -1, jnp.int32), tile[:-1]])
    first = (tile != prev_tile).astype(jnp.int32)
    prev_exp = jnp.concatenate([jnp.full((1,), -1, jnp.int32), exp[:-1].astype(jnp.int32)])
    newexp = (exp != prev_exp).astype(jnp.int32)
    parity = (jnp.cumsum(newexp) - 1) % 2
    ids = jnp.arange(N_EXPERTS, dtype=jnp.int32)
    later = (ids[None, :] > ids[:, None]) & (counts[None, :] > 0)
    next_of = jnp.min(jnp.where(later, ids[None, :], N_EXPERTS), axis=1)
    next_of = jnp.where(next_of == N_EXPERTS, -1, next_of)
    i32 = lambda v: v.astype(jnp.int32)
    return start.astype(jnp.int32), (i32(tile), i32(exp), i32(lo), i32(hi), first, i32(total).reshape(1),
                                     newexp, i32(pick(next_of)), i32(parity))


def moe_sublayer(x, t, mods, ln_g, ln_b, layout, layer, w_router, router_bias, w_gate, w_up, w_down,
                 ws_gate, ws_up, ws_down):
    tt, dm = t.shape
    eidx, pos, gate, counts = route(t, w_router, router_bias)
    start, items = _ffn_items(counts, tt * TOP_K)
    ids = jnp.arange(N_EXPERTS, dtype=jnp.int32)
    slot = pos + jnp.sum(jnp.where(eidx[..., None] == ids, start, 0), axis=-1)
    xs = dispatch(t, slot)
    ys = expert_ffn(xs, items, w_gate, w_up, w_down, layer)
    shared = shared_ffn(t, ws_gate, ws_up, ws_down)
    gate_cols = jnp.swapaxes(gate, 1, 2).reshape(tt, 8)
    return combine(shared, ys, gate_cols, slot, x, mods, ln_g, ln_b, layout)


TOKEN_TILE = 256
MOD_SHIFT1, MOD_SCALE1, MOD_GATE1, MOD_SHIFT2, MOD_SCALE2, MOD_GATE2 = range(6)


class Layout:
    def __init__(self, n_ctx_seq, ctx_len, n_lat_seq, lat_len):
        self.n_ctx_seq, self.ctx_len, self.n_lat_seq, self.lat_len = n_ctx_seq, ctx_len, n_lat_seq, lat_len
        self.n_ctx = n_ctx_seq * ctx_len
        self.n_tok = self.n_ctx + n_lat_seq * lat_len
        assert ctx_len % TOKEN_TILE == 0 and lat_len % TOKEN_TILE == 0

    def cond_row(self, tile):
        ctx_tiles = self.n_ctx // TOKEN_TILE
        return jnp.where(tile < ctx_tiles, 0, 1 + (tile - ctx_tiles) // (self.lat_len // TOKEN_TILE))

    def tile_pos(self, tile):
        ctx_tiles = self.n_ctx // TOKEN_TILE
        per_ctx, per_lat = self.ctx_len // TOKEN_TILE, self.lat_len // TOKEN_TILE
        is_ctx = tile < ctx_tiles
        return (jnp.where(is_ctx, tile % per_ctx, (tile - ctx_tiles) % per_lat),
                jnp.where(is_ctx, per_ctx, per_lat))


def _premix_kernel(x_ref, prev_ref, next_ref, mod_ref, mp_ref, mn_ref, *out_refs, layout):
    tm = x_ref.shape[0]
    pos, per_seq = layout.tile_pos(pl.program_id(0))
    shift, scale = mod_ref[MOD_SHIFT1:MOD_SHIFT1 + 1, :], mod_ref[MOD_SCALE1:MOD_SCALE1 + 1, :]
    mod = lambda t: t * (1.0 + scale) + shift
    h = mod(x_ref[...])
    h_before = jnp.where(pos > 0, mod(prev_ref[...]), 0.0)
    h_after = jnp.where(pos < per_seq - 1, mod(next_ref[...]), 0.0)
    row = lax.broadcasted_iota(jnp.int32, h.shape, 0)
    d_prev = jnp.where(row == 0, h_before, pltpu.roll(h, 1, 0)) - h
    d_next = jnp.where(row == tm - 1, h_after, pltpu.roll(h, tm - 1, 0)) - h
    for i, o_ref in enumerate(out_refs):
        o_ref[...] = (h + mp_ref[i:i + 1, :] * d_prev + mn_ref[i:i + 1, :] * d_next).astype(BF16)


def premix(x, mods, mix_prev, mix_next, layout):
    tt, dm = x.shape
    nt = tt // TOKEN_TILE
    zero = jnp.zeros((1, dm), F32)
    prev_rows = jnp.concatenate([zero, x[TOKEN_TILE - 1::TOKEN_TILE][:-1]], axis=0)[:, None, :]
    next_rows = jnp.concatenate([x[::TOKEN_TILE][1:], zero], axis=0)[:, None, :]
    tile = pl.BlockSpec((TOKEN_TILE, dm), lambda i: (i, 0))
    edge = pl.BlockSpec((None, 1, dm), lambda i: (i, 0, 0))
    par = pl.BlockSpec((6, dm), lambda i: (0, 0))
    return pl.pallas_call(
        functools.partial(_premix_kernel, layout=layout),
        out_shape=[jax.ShapeDtypeStruct((tt, dm), BF16)] * 6,
        grid=(nt,),
        in_specs=[tile, edge, edge, pl.BlockSpec((None, 6, dm), lambda i: (layout.cond_row(i), 0, 0)), par, par],
        out_specs=[tile] * 6,
        compiler_params=pltpu.CompilerParams(
            dimension_semantics=("arbitrary",), vmem_limit_bytes=VMEM_LIMIT_BYTES),
        name="premix",
    )(x, prev_rows, next_rows, mods, mix_prev, mix_next)


def _residual_ln(x, y, gate, g, b):
    z = ALPHA * x + gate * y
    mu = jnp.mean(z, axis=-1, keepdims=True)
    zc = z - mu
    var = jnp.mean(zc * zc, axis=-1, keepdims=True)
    return zc * lax.rsqrt(var + LN_EPS) * g + b


def _post_kernel(x_ref, y_ref, mod_ref, g_ref, b_ref, *out_refs, gate_row, mod_rows):
    xn = _residual_ln(x_ref[...], y_ref[...], mod_ref[gate_row:gate_row + 1, :], g_ref[...], b_ref[...])
    out_refs[0][...] = xn
    if mod_rows is not None:
        sh, sc = mod_rows
        out_refs[1][...] = xn * (1.0 + mod_ref[sc:sc + 1, :]) + mod_ref[sh:sh + 1, :]


def post(x, y, mods, ln_g, ln_b, layout, gate_row, mod_rows):
    tt, dm = x.shape
    tile = pl.BlockSpec((TOKEN_TILE, dm), lambda i: (i, 0))
    vec = pl.BlockSpec((1, dm), lambda i: (0, 0))
    n_out = 1 if mod_rows is None else 2
    return pl.pallas_call(
        functools.partial(_post_kernel, gate_row=gate_row, mod_rows=mod_rows),
        out_shape=[jax.ShapeDtypeStruct((tt, dm), F32)] * n_out,
        grid=(tt // TOKEN_TILE,),
        in_specs=[tile, tile, pl.BlockSpec((None, 6, dm), lambda i: (layout.cond_row(i), 0, 0)), vec, vec],
        out_specs=[tile] * n_out,
        compiler_params=pltpu.CompilerParams(
            dimension_semantics=("arbitrary",), vmem_limit_bytes=VMEM_LIMIT_BYTES),
        name="post",
    )(x, y, mods, ln_g[None, :], ln_b[None, :])


def _pool_kernel(x_ref, mod_ref, w_ref, scale_ref, o_ref, *, grid_w):
    seq_len = x_ref.shape[0]
    t_idx = lax.broadcasted_iota(jnp.int32, x_ref.shape, 0)

    def shifted(v, k, pos, extent, stride):
        if k > 0:
            return jnp.where(pos < extent - k, pltpu.roll(v, seq_len - k * stride, 0), 0.0)
        return jnp.where(pos >= -k, pltpu.roll(v, -k * stride, 0), 0.0)

    def box(v, w, pos, extent, stride):
        m = w // 2
        ahead, behind, step = v, v, 1
        while step < m:
            ahead = ahead + shifted(ahead, step, pos, extent, stride)
            behind = behind + shifted(behind, -step, pos, extent, stride)
            step *= 2
        total = ahead + shifted(behind, -1, pos, extent, stride)
        cnt = jnp.minimum(pos + m, extent) - jnp.maximum(pos - m, 0)
        return total, cnt.astype(F32)

    h = x_ref[...] * (1.0 + mod_ref[MOD_SCALE1:MOD_SCALE1 + 1, :]) + mod_ref[MOD_SHIFT1:MOD_SHIFT1 + 1, :]
    for gi, w in enumerate(POOL_WINDOWS):
        @pl.when(pl.program_id(1) == gi)
        def _(w=w):
            if grid_w is None:
                s, cnt = box(h, w, t_idx, seq_len, 1)
            else:
                s, cr = box(h, w, t_idx // grid_w, seq_len // grid_w, grid_w)
                s, cc = box(s, w, t_idx % grid_w, grid_w, 1)
                cnt = cr * cc
            p = (s / cnt - h).astype(BF16)
            o_ref[...] = _dot(p, w_ref[...].astype(BF16)) * scale_ref[...]


def pool_mix(x, mods, pool_w, pool_scale, layout, latent):
    dm = x.shape[1]
    n_grp = len(POOL_WINDOWS)
    if latent:
        n_seq, seq_len, first_blk, cond0 = layout.n_lat_seq, layout.lat_len, layout.n_ctx // layout.lat_len, 1
    else:
        n_seq, seq_len, first_blk, cond0 = layout.n_ctx_seq, layout.ctx_len, 0, None
    cond = (lambda i: 0) if cond0 is None else (lambda i: cond0 + i)
    return pl.pallas_call(
        functools.partial(_pool_kernel, grid_w=GRID_W if latent else None),
        out_shape=jax.ShapeDtypeStruct((n_seq * seq_len, dm), F32),
        grid=(n_seq, n_grp),
        in_specs=[pl.BlockSpec((seq_len, POOL_GROUP), lambda i, g: (first_blk + i, g)),
                  pl.BlockSpec((None, 6, POOL_GROUP), lambda i, g: (cond(i), 0, g)),
                  pl.BlockSpec((None, POOL_GROUP, POOL_GROUP), lambda i, g: (g, 0, 0)),
                  pl.BlockSpec((1, POOL_GROUP), lambda i, g: (0, g))],
        out_specs=pl.BlockSpec((seq_len, POOL_GROUP), lambda i, g: (i, g)),
        compiler_params=pltpu.CompilerParams(
            dimension_semantics=("arbitrary", "arbitrary"), vmem_limit_bytes=VMEM_LIMIT_BYTES),
        name="pool_mix",
    )(x, mods, pool_w, pool_scale[None, :])


def kernel(x_prompt, x_sample, c, state_rwkv, c_ctx, w_ada, b_ada, ln_g, ln_b, rw_mix_prev, rw_mix_next, rw_w_r, rw_w_k, rw_w_v, rw_w_o, rw_w0, rw_w1, rw_w2, rw_a0, rw_a1, rw_a2, rw_g1, rw_g2, rw_k_k, rw_k_a, rw_r_k, rw_gn_g, rw_gn_b, pool_w, pool_scale, moe_router, moe_router_bias, moe_w_gate, moe_w_up, moe_w_down, moe_ws_gate, moe_ws_up, moe_ws_down):
    dm = D_MODEL
    H, N = RWKV_HEADS, RWKV_HEAD
    bc, lc_, _ = x_prompt.shape
    bl, ll, _ = x_sample.shape
    layout = Layout(bc, lc_, bl, ll)
    tc = layout.n_ctx
    x = jnp.concatenate([x_prompt.reshape(tc, dm), x_sample.reshape(bl * ll, dm)], axis=0)
    cond = jnp.concatenate([c_ctx[None, :], c], axis=0)
    n_cond = cond.shape[0]
    assert n_cond <= 8
    cond_pad = jnp.zeros((8, dm), F32).at[:n_cond].set(jax.nn.silu(cond))
    new_states = []

    for l in range(DEPTH):
        mods = (matmul(cond_pad, w_ada, tn=1024, layer=l) + b_ada[l]).reshape(8, 6, dm)
        j = l // 2
        if l % 2 == 0:
            xr, xw, xk, xv, xa, xg = premix(x, mods, rw_mix_prev[j], rw_mix_next[j], layout)
            r = matmul(xr, rw_w_r[j])
            k = matmul(xk, rw_w_k[j])
            v = matmul(xv, rw_w_v[j])
            g = lora(xg, rw_g1[j], rw_g2[j], jax.nn.sigmoid)
            wl = [lora(xw, rw_w1[j, d], rw_w2[j, d], jnp.tanh) for d in range(2)]
            al = [lora(xa, rw_a1[j, d], rw_a2[j, d]) for d in range(2)]
            par = jnp.zeros((16, dm), F32).at[:9].set(jnp.stack(
                [rw_w0[j, 0], rw_w0[j, 1], rw_a0[j, 0], rw_a0[j, 1], rw_k_k[j], rw_k_a[j],
                 rw_r_k[j].reshape(dm), rw_gn_g[j], rw_gn_b[j]]))
            out_c, s_fin = rwkv_core(r, k, v, g, wl, al, par, jnp.zeros((bc, 2, H, N, N), F32), 0, lc_)
            out_l, _ = rwkv_core(r, k, v, g, wl, al, par, state_rwkv[:, j], tc, ll)
            new_states.append(s_fin)
            mix = matmul(jnp.concatenate([out_c, out_l], axis=0), rw_w_o[j])
        else:
            mix = jnp.concatenate([pool_mix(x, mods, pool_w[j], pool_scale[j], layout, False),
                                   pool_mix(x, mods, pool_w[j], pool_scale[j], layout, True)], axis=0)
        x, h2 = post(x, mix, mods, ln_g[l, 0], ln_b[l, 0], layout, MOD_GATE1, (MOD_SHIFT2, MOD_SCALE2))
        x = moe_sublayer(x, h2, mods, ln_g[l, 1], ln_b[l, 1], layout, l, moe_router[l], moe_router_bias[l],
                         moe_w_gate, moe_w_up, moe_w_down, moe_ws_gate[l], moe_ws_up[l], moe_ws_down[l])

    return (x[:tc].reshape(bc, lc_, dm), x[tc:].reshape(bl, ll, dm), jnp.stack(new_states, axis=1))
```

```python
import functools

import jax
import jax.numpy as jnp
from jax import lax
from jax.experimental import pallas as pl
from jax.experimental.pallas import tpu as pltpu

D_MODEL = 2048
DEPTH = 2
GRID_W = 64
RWKV_HEAD = 64
RWKV_HEADS = D_MODEL // RWKV_HEAD
POOL_WINDOWS = (2, 4, 8, 16)
POOL_GROUP = D_MODEL // len(POOL_WINDOWS)
N_EXPERTS = 64
TOP_K = 6
N_GROUPS = 8
TOPK_GROUPS = 4
ROUTED_SCALE = 2.5
ALPHA = (2 * DEPTH) ** 0.25
LN_EPS = 1e-5
GN_EPS = 64e-5

F32 = jnp.float32
BF16 = jnp.bfloat16

VMEM_LIMIT_BYTES = 48 * 1024 * 1024
FFN_VMEM_LIMIT_BYTES = 56 * 1024 * 1024

SCAN_CHUNK = 64
SCAN_HEADS = 4
SCAN_LANES = SCAN_HEADS * RWKV_HEAD
SCAN_BLOCK_ROWS = 1024
SCAN_PREP_CHUNKS = 4
MOE_TILE = 256


def _dot(a, b):
    return jnp.dot(a, b, preferred_element_type=F32)


def _dot_nt(a, b):
    return lax.dot_general(a, b, (((1,), (1,)), ((), ())), preferred_element_type=F32)


def _dot_tn(a, b):
    return lax.dot_general(a, b, (((0,), (0,)), ((), ())), preferred_element_type=F32)


def _mm_kernel(x_ref, w_ref, o_ref):
    o_ref[...] = _dot(x_ref[...].astype(BF16), w_ref[...].astype(BF16))


def matmul(x, w, *, tm=1024, tn=512, layer=None):
    m, k = x.shape
    n = w.shape[-1]
    tm = min(tm, m)
    tn = min(tn, n)
    assert m % tm == 0 and n % tn == 0
    if layer is None:
        w_spec = pl.BlockSpec((k, tn), lambda j, i: (0, j))
    else:
        w_spec = pl.BlockSpec((None, k, tn), lambda j, i: (layer, 0, j))
    return pl.pallas_call(
        _mm_kernel,
        out_shape=jax.ShapeDtypeStruct((m, n), F32),
        grid=(n // tn, m // tm),
        in_specs=[pl.BlockSpec((tm, k), lambda j, i: (i, 0)), w_spec],
        out_specs=pl.BlockSpec((tm, tn), lambda j, i: (i, j)),
        compiler_params=pltpu.CompilerParams(
            dimension_semantics=("arbitrary", "arbitrary"),
            vmem_limit_bytes=VMEM_LIMIT_BYTES),
        name="matmul",
    )(x, w)


def _lora_kernel(x_ref, w1_ref, w2_ref, o_ref, *, act):
    h = _dot(x_ref[...].astype(BF16), w1_ref[...].astype(BF16))
    if act is not None:
        h = act(h)
    o_ref[...] = _dot(h.astype(BF16), w2_ref[...].astype(BF16))


def lora(x, w1, w2, act=None, *, tm=1024):
    m, k = x.shape
    r = w1.shape[1]
    n = w2.shape[1]
    assert m % tm == 0
    return pl.pallas_call(
        functools.partial(_lora_kernel, act=act),
        out_shape=jax.ShapeDtypeStruct((m, n), F32),
        grid=(m // tm,),
        in_specs=[pl.BlockSpec((tm, k), lambda i: (i, 0)), pl.BlockSpec((k, r), lambda i: (0, 0)),
                  pl.BlockSpec((r, n), lambda i: (0, 0))],
        out_specs=pl.BlockSpec((tm, n), lambda i: (i, 0)),
        compiler_params=pltpu.CompilerParams(
            dimension_semantics=("arbitrary",), vmem_limit_bytes=VMEM_LIMIT_BYTES),
        name="lora",
    )(x, w1, w2)


def _split3(x):
    hi = x.astype(BF16)
    r1 = x - hi.astype(F32)
    mid = r1.astype(BF16)
    lo = (r1 - mid.astype(F32)).astype(BF16)
    return hi, mid, lo


def _softplus(z):
    return jnp.maximum(z, 0.0) + jnp.log(1.0 + jnp.exp(-jnp.abs(z)))


def _scan_kernel(r_ref, k_ref, v_ref, g_ref, wl0_ref, wl1_ref, al0_ref, al1_ref, par_ref, s0_ref,
                 out_ref, sfin_ref, s_scr, kk_ref, lw_ref, a_ref, kd_ref, bonus_ref, y_ref,
                 pr_s, ak_s, lra_s, uy_s, gend_s, lvl_s, *, nc, nseq, prep):
    C, N, HB, W = SCAN_CHUNK, RWKV_HEAD, SCAN_HEADS, SCAN_LANES
    t_idx = lax.broadcasted_iota(jnp.int32, (C, W), 0)
    s_idx = lax.broadcasted_iota(jnp.int32, (C, W), 1) % C
    eye_row = (s_idx == t_idx).astype(F32)
    tri_t = lax.broadcasted_iota(jnp.int32, (C, C), 0)
    tri_s = lax.broadcasted_iota(jnp.int32, (C, C), 1)
    blk_r = lax.broadcasted_iota(jnp.int32, (W, W), 0) // N
    blk_c = lax.broadcasted_iota(jnp.int32, (W, W), 1) // N
    bd_mask = blk_r == blk_c
    bd_mask_bf = bd_mask.astype(BF16)

    def expand(x_row_bf):
        return jnp.concatenate([x_row_bf] * HB, axis=0) * bd_mask_bf

    n_chunks = nseq * nc
    for s in range(nseq):
        for d in range(2):
            s_scr[2 * s + d] = jnp.zeros((W, W), F32)
            for h in range(HB):
                s_scr[2 * s + d, h * N:(h + 1) * N, h * N:(h + 1) * N] = s0_ref[s, d, h]
    y_ref[...] = jnp.zeros_like(y_ref)

    levels = []
    b = 2
    while b < C:
        levels.append(b)
        b *= 2

    @pl.when((pl.program_id(0) == 0) & (pl.program_id(1) == 0))
    def _():
        t_bd = lax.broadcasted_iota(jnp.int32, (W, W), 0) % C
        s_bd = lax.broadcasted_iota(jnp.int32, (W, W), 1) % C
        for li, b in enumerate(levels):
            tb, sb = t_bd // b, s_bd // b
            lvl_s[0, li] = (bd_mask & (tb % 2 == 1) & (sb == tb - 1)).astype(BF16)
            lvl_s[1, li] = (bd_mask & (tb % 2 == 0) & (sb == tb + 1)).astype(BF16)

    ones_bd = bd_mask_bf
    RB = 4 * C
    par = lambda i: par_ref[i:i + 1, :]

    def head_sum(x):
        hi, lo = _split2(x)
        return _dot(hi, ones_bd) + _dot(lo, ones_bd)

    def prologue(i, carry):
        rows = pl.ds(pl.multiple_of(i * RB, RB), RB)
        k = k_ref[rows, :]
        kkp = k * par(4)
        kk_ref[rows, :] = kkp * lax.rsqrt(jnp.maximum(head_sum(kkp * kkp), 1e-24))
        kd_sum = jnp.zeros_like(k)
        for d, (wl_ref, al_ref) in enumerate(((wl0_ref, al0_ref), (wl1_ref, al1_ref))):
            w_log = -_softplus(-(par(d) + wl_ref[rows, :])) - 0.5
            lw_ref[d, rows, :] = -jnp.exp(w_log)
            a = jax.nn.sigmoid(par(2 + d) + al_ref[rows, :])
            a_ref[d, rows, :] = a
            kd = k * (1.0 + (a - 1.0) * par(5))
            kd_ref[d, rows, :] = kd
            kd_sum = kd_sum + kd
        bonus_ref[rows, :] = head_sum(r_ref[rows, :] * kd_sum * par(6)) * v_ref[rows, :]
        return carry

    lax.fori_loop(0, nseq * nc * C // RB, prologue, 0)

    def interleave(chains):
        live = list(chains)
        while live:
            nxt = []
            for ch in live:
                try:
                    next(ch)
                    nxt.append(ch)
                except StopIteration:
                    pass
            live = nxt

    def prepare_chunk(d, ci):
        fwd = d == 0
        strict = (s_idx < t_idx) if fwd else (s_idx > t_idx)
        incl = (s_idx <= t_idx) if fwd else (s_idx >= t_idx)
        tri = ((tri_s <= tri_t) if fwd else (tri_s >= tri_t)).astype(BF16)
        rows = pl.ds(pl.multiple_of(ci * C, C), C)
        lw = lw_ref[d, rows, :]
        hi, mid, lo = _split3(lw)
        lc = _dot(tri, hi) + _dot(tri, mid) + _dot(tri, lo)
        yield
        kk = kk_ref[rows, :]
        g_in = jnp.exp(lc)
        g_inv = jnp.exp(-lc)
        g_ex = jnp.exp(lc - lw)
        last = C - 1 if fwd else 0
        gend_s[d, ci] = jnp.broadcast_to(g_in[last:last + 1, :], (8, W))
        bt = (kk * g_ex).astype(BF16)
        rt = (r_ref[rows, :] * g_in).astype(BF16)
        at = (-(kk * a_ref[d, rows, :]) * g_inv).astype(BF16)
        kt = (kd_ref[d, rows, :] * g_inv).astype(BF16)
        br = jnp.concatenate([bt, rt], axis=0)
        ak_s[d, ci] = jnp.concatenate([at, kt], axis=0)
        la = _dot_nt(br, expand(at))
        lk = _dot_nt(br, expand(kt))
        yield
        lba = la[:C]
        lra_s[d, ci] = jnp.where(incl, la[C:], 0.0).astype(BF16)
        lbrk = jnp.concatenate([jnp.where(strict, lk[:C], 0.0), jnp.where(incl, lk[C:], 0.0)], axis=0)
        wy2 = _dot(lbrk.astype(BF16), expand(v_ref[rows, :].astype(BF16)))

        pair = (s_idx == t_idx - 1) if fwd else (s_idx == t_idx + 1)
        inv = eye_row + jnp.where(pair & (t_idx % 2 == (1 if fwd else 0)), lba, 0.0)
        lba4 = jnp.concatenate([lba.astype(BF16)] * HB, axis=0)
        for li in range(len(levels)):
            m1 = _dot(inv.astype(BF16), lba4 * lvl_s[d, li])
            yield
            inv = inv + _dot(m1.astype(BF16), expand(inv.astype(BF16)))
            yield
        inv_bf = inv.astype(BF16)
        pu = _dot(inv_bf, expand(bt))
        u0 = _dot(inv_bf, expand(wy2[:C].astype(BF16)))
        yield
        pr_s[d, ci] = jnp.concatenate([pu.astype(BF16), rt], axis=0)
        uy_s[d, ci] = jnp.concatenate([u0, wy2[C:]], axis=0)

    def prepare(i, carry):
        interleave([prepare_chunk(d, i * prep + j) for j in range(prep) for d in range(2)])
        return carry

    lax.fori_loop(0, n_chunks // prep, prepare, 0)

    def advance_chunk(s, d, ci):
        rows = pl.ds(pl.multiple_of(ci * C, C), C)
        s_bd = s_scr[2 * s + d]
        py = _dot_nt(pr_s[d, ci], s_bd.astype(BF16))
        yield
        uy = uy_s[d, ci]
        u_bf = (py[:C] + uy[:C]).astype(BF16)
        upd = _dot_tn(jnp.concatenate([u_bf, v_ref[rows, :].astype(BF16)], axis=0), ak_s[d, ci])
        y = py[C:] + uy[C:] + _dot(lra_s[d, ci], expand(u_bf))
        yield
        y_ref[rows, :] += y
        s_scr[2 * s + d] = jnp.where(bd_mask, (s_bd + upd) * gend_s[d, ci][0:1, :], 0.0)

    def advance(c, carry):
        interleave([advance_chunk(s, d, s * nc + (c if d == 0 else nc - 1 - c))
                    for s in range(nseq) for d in range(2)])
        return carry

    lax.fori_loop(0, nc, advance, 0)

    for s in range(nseq):
        for d in range(2):
            for h in range(HB):
                sfin_ref[s, d, h] = s_scr[2 * s + d, h * N:(h + 1) * N, h * N:(h + 1) * N]

    def epilogue(i, carry):
        rows = pl.ds(pl.multiple_of(i * RB, RB), RB)
        y = y_ref[rows, :]
        yc = y - head_sum(y) * (1.0 / N)
        var = head_sum(yc * yc) * (1.0 / N)
        yn = yc * lax.rsqrt(var + GN_EPS) * par(7) + par(8)
        out_ref[rows, :] = ((yn + bonus_ref[rows, :]) * g_ref[rows, :]).astype(BF16)
        return carry

    lax.fori_loop(0, nseq * nc * C // RB, epilogue, 0)


def rwkv_core(r, k, v, g, wl, al, par, s0, first_row, seq_len):
    b = s0.shape[0]
    d = r.shape[1]
    nseq = max(1, min(b, SCAN_BLOCK_ROWS // seq_len))
    l = nseq * seq_len
    assert first_row % l == 0 and b % nseq == 0
    nc = seq_len // SCAN_CHUNK
    first_blk = first_row // l
    seq = pl.BlockSpec((l, SCAN_LANES), lambda i, j: (first_blk + i, j))
    oseq = pl.BlockSpec((l, SCAN_LANES), lambda i, j: (i, j))
    st = pl.BlockSpec((nseq, 2, SCAN_HEADS, RWKV_HEAD, RWKV_HEAD), lambda i, j: (i, 0, j, 0, 0))
    seq_scr = pltpu.VMEM((l, SCAN_LANES), F32)
    seq2_scr = pltpu.VMEM((2, l, SCAN_LANES), F32)
    C, W = SCAN_CHUNK, SCAN_LANES
    per_chunk = lambda rows, dt: pltpu.VMEM((2, nseq * nc, rows, W), dt)
    return pl.pallas_call(
        functools.partial(_scan_kernel, nc=nc, nseq=nseq, prep=SCAN_PREP_CHUNKS),
        out_shape=(jax.ShapeDtypeStruct((b * seq_len, d), BF16), jax.ShapeDtypeStruct(s0.shape, F32)),
        grid=(b // nseq, d // SCAN_LANES),
        in_specs=[seq] * 8 + [pl.BlockSpec((16, SCAN_LANES), lambda i, j: (0, j)), st],
        out_specs=(oseq, st),
        scratch_shapes=[pltpu.VMEM((2 * nseq, SCAN_LANES, SCAN_LANES), F32),
                        seq_scr, seq2_scr, seq2_scr, seq2_scr, seq_scr, seq_scr,
                        per_chunk(2 * C, BF16), per_chunk(2 * C, BF16), per_chunk(C, BF16),
                        per_chunk(2 * C, F32), per_chunk(8, F32),
                        pltpu.VMEM((2, 5, W, W), BF16)],
        compiler_params=pltpu.CompilerParams(
            dimension_semantics=("arbitrary", "arbitrary"),
            vmem_limit_bytes=VMEM_LIMIT_BYTES),
        name="wkv_scan",
    )(r, k, v, g, wl[0], wl[1], al[0], al[1], par, s0)


def _split2(x):
    hi = x.astype(BF16)
    return hi, (x - hi.astype(F32)).astype(BF16)


def _first_index(hit_value, iota, sentinel, axis):
    return jnp.min(jnp.where(hit_value, iota, sentinel), axis=axis, keepdims=True)


def _router_kernel(h_ref, wrt_ref, bias_ref, eidx_ref, pos_ref, gate_ref, cnt_ref, cnt_scr):
    tm = h_ref.shape[0]
    E, G, GS = N_EXPERTS, N_GROUPS, N_EXPERTS // N_GROUPS
    neg = -jnp.inf

    @pl.when(pl.program_id(0) == 0)
    def _():
        cnt_scr[...] = jnp.zeros_like(cnt_scr)

    x_hi, x_lo = _split2(h_ref[...])
    w_hi, w_lo = _split2(wrt_ref[...])
    logits = _dot_nt(w_hi, x_hi) + (_dot_nt(w_hi, x_lo) + _dot_nt(w_lo, x_hi))
    scores = jax.nn.sigmoid(logits)
    biased = scores + bias_ref[...]

    g3 = biased.reshape(G, GS, tm)
    w_iota = lax.broadcasted_iota(jnp.int32, (G, GS, tm), 1).astype(F32)
    m1 = jnp.max(g3, axis=1, keepdims=True)
    first = _first_index(g3 == m1, w_iota, float(GS), 1)
    m2 = jnp.max(jnp.where(w_iota == first, neg, g3), axis=1, keepdims=True)
    gs = (m1 + m2).reshape(G, tm)

    g_iota = lax.broadcasted_iota(jnp.int32, (G, tm), 0).astype(F32)
    gsel = jnp.zeros((G, tm), jnp.bool_)
    cur = gs
    for _ in range(TOPK_GROUPS):
        m = jnp.max(cur, axis=0, keepdims=True)
        hit = g_iota == _first_index(cur == m, g_iota, float(G), 0)
        gsel = gsel | hit
        cur = jnp.where(hit, neg, cur)

    masked = jnp.where(gsel[:, None, :], g3, neg).reshape(E, tm)
    e_iota = lax.broadcasted_iota(jnp.int32, (E, tm), 0).astype(F32)
    sel = jnp.zeros((E, tm), jnp.bool_)
    hits, ids = [], []
    cur = masked
    for _ in range(TOP_K):
        m = jnp.max(cur, axis=0, keepdims=True)
        f = _first_index(cur == m, e_iota, float(E), 0)
        hit = e_iota == f
        hits.append(hit)
        ids.append(f)
        sel = sel | hit
        cur = jnp.where(hit, neg, cur)

    ssum = jnp.sum(jnp.where(sel, scores, 0.0), axis=0, keepdims=True)
    gates = scores / (ssum + 1e-20) * ROUTED_SCALE

    before = (lax.broadcasted_iota(jnp.int32, (tm, tm), 0) < lax.broadcasted_iota(jnp.int32, (tm, tm), 1))
    sel_f = sel.astype(F32)
    rank = _dot(sel_f.astype(BF16), before.astype(BF16)) + cnt_scr[:, 0:1]
    cnt_new = cnt_scr[...] + jnp.sum(sel_f, axis=1, keepdims=True)
    cnt_scr[...] = cnt_new
    cnt_ref[...] = cnt_new.astype(jnp.int32)

    k_iota = lax.broadcasted_iota(jnp.int32, (8, tm), 0)
    eidx8 = jnp.zeros((8, tm), F32)
    pos8 = jnp.zeros((8, tm), F32)
    gate8 = jnp.zeros((8, tm), F32)
    for k in range(TOP_K):
        row = k_iota == k
        eidx8 = jnp.where(row, ids[k], eidx8)
        pos8 = jnp.where(row, jnp.sum(jnp.where(hits[k], rank, 0.0), axis=0, keepdims=True), pos8)
        gate8 = jnp.where(row, jnp.sum(jnp.where(hits[k], gates, 0.0), axis=0, keepdims=True), gate8)
    eidx_ref[...] = eidx8.astype(jnp.int32)
    pos_ref[...] = pos8.astype(jnp.int32)
    gate_ref[...] = gate8


def route(h, w_router, router_bias):
    tt, dm = h.shape
    nt = tt // MOE_TILE
    tok = pl.BlockSpec((None, 8, MOE_TILE), lambda i: (i, 0, 0))
    eidx, pos, gate, cnt = pl.pallas_call(
        _router_kernel,
        out_shape=(jax.ShapeDtypeStruct((nt, 8, MOE_TILE), jnp.int32),
                   jax.ShapeDtypeStruct((nt, 8, MOE_TILE), jnp.int32),
                   jax.ShapeDtypeStruct((nt, 8, MOE_TILE), F32),
                   jax.ShapeDtypeStruct((N_EXPERTS, 128), jnp.int32)),
        grid=(nt,),
        in_specs=[pl.BlockSpec((MOE_TILE, dm), lambda i: (i, 0)),
                  pl.BlockSpec((N_EXPERTS, dm), lambda i: (0, 0)),
                  pl.BlockSpec((N_EXPERTS, 1), lambda i: (0, 0))],
        out_specs=(tok, tok, tok, pl.BlockSpec((N_EXPERTS, 128), lambda i: (0, 0))),
        scratch_shapes=[pltpu.VMEM((N_EXPERTS, 128), F32)],
        compiler_params=pltpu.CompilerParams(
            dimension_semantics=("arbitrary",), vmem_limit_bytes=VMEM_LIMIT_BYTES),
        name="moe_router",
    )(h, w_router.T, router_bias[:, None])
    return eidx, pos, gate, cnt[:, 0]


def _dispatch_kernel(slot_ref, x_ref, xs_ref, sem):
    tm = x_ref.shape[0]

    def issue(t, carry):
        for k in range(TOP_K):
            pltpu.make_async_copy(x_ref.at[pl.ds(t, 1), :], xs_ref.at[pl.ds(slot_ref[k, t], 1), :],
                                  sem).start(priority=k % 2)
        return carry

    lax.fori_loop(0, tm, issue, 0, unroll=4)
    for k in range(TOP_K):
        pltpu.make_async_copy(x_ref, xs_ref.at[pl.ds(0, tm), :], sem).wait()


def dispatch(h, slot):
    tt, dm = h.shape
    nt = tt // MOE_TILE
    return pl.pallas_call(
        _dispatch_kernel,
        out_shape=jax.ShapeDtypeStruct((tt * TOP_K, dm), F32),
        grid=(nt,),
        in_specs=[pl.BlockSpec((None, 8, MOE_TILE), lambda i: (i, 0, 0), memory_space=pltpu.SMEM),
                  pl.BlockSpec((MOE_TILE, dm), lambda i: (i, 0))],
        out_specs=pl.BlockSpec(memory_space=pl.ANY),
        scratch_shapes=[pltpu.SemaphoreType.DMA],
        compiler_params=pltpu.CompilerParams(
            dimension_semantics=("arbitrary",), vmem_limit_bytes=VMEM_LIMIT_BYTES),
        name="moe_dispatch",
    )(slot, h)


def _combine_kernel(slot_ref, slot_nxt_ref, gate_ref, base_ref, x_ref,
                    mod_ref, g_ref, b_ref, ys_ref, o_ref, buf, sems):
    tm = base_ref.shape[0]
    i = pl.program_id(0)
    n = pl.num_programs(0)

    def gather(s_ref, slot_buf):
        def issue(t, carry):
            for k in range(TOP_K):
                pltpu.make_async_copy(ys_ref.at[pl.ds(s_ref[k, t], 1), :], buf.at[slot_buf, k, pl.ds(t, 1), :],
                                      sems.at[slot_buf]).start(priority=k % 2)
            return carry
        lax.fori_loop(0, tm, issue, 0, unroll=4)

    cur = i % 2
    for slot_buf in range(2):
        @pl.when((i == 0) & (cur == slot_buf))
        def _(slot_buf=slot_buf):
            gather(slot_ref, slot_buf)

        @pl.when((i + 1 < n) & (cur != slot_buf))
        def _(slot_buf=slot_buf):
            gather(slot_nxt_ref, slot_buf)

    for slot_buf in range(2):
        @pl.when(cur == slot_buf)
        def _(slot_buf=slot_buf):
            for k in range(TOP_K):
                pltpu.make_async_copy(ys_ref.at[pl.ds(0, tm), :], buf.at[slot_buf, k], sems.at[slot_buf]).wait()
            acc = base_ref[...]
            for k in range(TOP_K):
                acc = acc + gate_ref[:, k:k + 1] * buf[slot_buf, k]
            o_ref[...] = _residual_ln(x_ref[...], acc, mod_ref[MOD_GATE2:MOD_GATE2 + 1, :], g_ref[...], b_ref[...])


def combine(base, ys, gate_cols, slot, x, mods, ln_g, ln_b, layout):
    tt, dm = base.shape
    nt = tt // MOE_TILE
    smem_tok = pl.BlockSpec((None, 8, MOE_TILE), lambda i: (i, 0, 0), memory_space=pltpu.SMEM)
    smem_nxt = pl.BlockSpec((None, 8, MOE_TILE), lambda i: (jnp.minimum(i + 1, nt - 1), 0, 0),
                            memory_space=pltpu.SMEM)
    row = pl.BlockSpec((MOE_TILE, dm), lambda i: (i, 0))
    vec = pl.BlockSpec((1, dm), lambda i: (0, 0))
    return pl.pallas_call(
        _combine_kernel,
        out_shape=jax.ShapeDtypeStruct((tt, dm), F32),
        grid=(nt,),
        in_specs=[smem_tok, smem_nxt, pl.BlockSpec((MOE_TILE, 8), lambda i: (i, 0)), row, row,
                  pl.BlockSpec((None, 6, dm), lambda i: (layout.cond_row(i), 0, 0)), vec, vec,
                  pl.BlockSpec(memory_space=pl.ANY)],
        out_specs=row,
        scratch_shapes=[pltpu.VMEM((2, TOP_K, MOE_TILE, dm), F32), pltpu.SemaphoreType.DMA((2,))],
        compiler_params=pltpu.CompilerParams(
            dimension_semantics=("arbitrary",), vmem_limit_bytes=VMEM_LIMIT_BYTES),
        name="moe_combine",
    )(slot, slot, gate_cols, base, x, mods, ln_g[None, :], ln_b[None, :], ys)


def _ffn_body(x, wg_ref, wu_ref, wd_ref, row_lo, row_hi):
    x = x.astype(BF16)
    h1 = _dot(x, wg_ref[...].astype(BF16))
    h2 = _dot(x, wu_ref[...].astype(BF16))
    h = (h1 * jax.nn.sigmoid(h1)) * h2
    if row_lo is not None:
        row = lax.broadcasted_iota(jnp.int32, h.shape, 0)
        h = jnp.where((row >= row_lo) & (row < row_hi), h, 0.0)
    return _dot(h.astype(BF16), wd_ref[...].astype(BF16))


def _ffn_items_kernel(tile_ref, exp_ref, lo_ref, hi_ref, first_ref, n_ref, newexp_ref, nxt_ref, par_ref,
                      x_ref, wg_hbm, wu_hbm, wd_hbm, o_ref, wg_buf, wu_buf, wd_buf, wg_bf, wu_bf, wd_bf, sems,
                      *, layer):
    del tile_ref
    w = pl.program_id(0)

    def weight_copies(e, buf):
        return (pltpu.make_async_copy(wg_hbm.at[layer, e], wg_buf.at[buf], sems.at[buf]),
                pltpu.make_async_copy(wu_hbm.at[layer, e], wu_buf.at[buf], sems.at[buf]),
                pltpu.make_async_copy(wd_hbm.at[layer, e], wd_buf.at[buf], sems.at[buf]))

    @pl.when(w == 0)
    def _():
        for cp in weight_copies(exp_ref[0], 0):
            cp.start()

    @pl.when((w < n_ref[0]) & (newexp_ref[w] == 1))
    def _():
        for buf in range(2):
            @pl.when(par_ref[w] == buf)
            def _(buf=buf):
                for cp in weight_copies(exp_ref[w], buf):
                    cp.wait()

                @pl.when(nxt_ref[w] >= 0)
                def _():
                    for cp in weight_copies(nxt_ref[w], 1 - buf):
                        cp.start()

                wg_bf[...] = wg_buf[buf].astype(BF16)
                wu_bf[...] = wu_buf[buf].astype(BF16)
                wd_bf[...] = wd_buf[buf].astype(BF16)

    @pl.when(w < n_ref[0])
    def _():
        y = _ffn_body(x_ref[...], wg_bf, wu_bf, wd_bf, lo_ref[w], hi_ref[w])

        @pl.when(first_ref[w] == 1)
        def _():
            o_ref[...] = y

        @pl.when(first_ref[w] == 0)
        def _():
            o_ref[...] += y


def expert_ffn(xs, items, wg, wu, wd, layer):
    rows, dm = xs.shape
    de = wg.shape[-1]
    n_items = items[0].shape[0]
    n_prefetch = len(items)

    def row_map(w, tile, *_):
        return (tile[w], 0)

    hbm = pl.BlockSpec(memory_space=pl.ANY)
    return pl.pallas_call(
        functools.partial(_ffn_items_kernel, layer=layer),
        out_shape=jax.ShapeDtypeStruct((rows, dm), F32),
        grid_spec=pltpu.PrefetchScalarGridSpec(
            num_scalar_prefetch=n_prefetch,
            grid=(n_items,),
            in_specs=[pl.BlockSpec((MOE_TILE, dm), row_map), hbm, hbm, hbm],
            out_specs=pl.BlockSpec((MOE_TILE, dm), row_map),
            scratch_shapes=[pltpu.VMEM((2, dm, de), F32), pltpu.VMEM((2, dm, de), F32),
                            pltpu.VMEM((2, de, dm), F32), pltpu.VMEM((dm, de), BF16),
                            pltpu.VMEM((dm, de), BF16), pltpu.VMEM((de, dm), BF16),
                            pltpu.SemaphoreType.DMA((2,))]),
        compiler_params=pltpu.CompilerParams(
            dimension_semantics=("arbitrary",), vmem_limit_bytes=FFN_VMEM_LIMIT_BYTES),
        name="expert_ffn",
    )(*items, xs, wg, wu, wd)


def _ffn_dense_kernel(x_ref, wg_ref, wu_ref, wd_ref, o_ref):
    o_ref[...] = _ffn_body(x_ref[...], wg_ref, wu_ref, wd_ref, None, None)


def shared_ffn(x, wg, wu, wd):
    tt, dm = x.shape
    de = wg.shape[-1]
    row = pl.BlockSpec((MOE_TILE, dm), lambda i: (i, 0))
    return pl.pallas_call(
        _ffn_dense_kernel,
        out_shape=jax.ShapeDtypeStruct((tt, dm), F32),
        grid=(tt // MOE_TILE,),
        in_specs=[row, pl.BlockSpec((dm, de), lambda i: (0, 0)), pl.BlockSpec((dm, de), lambda i: (0, 0)),
                  pl.BlockSpec((de, dm), lambda i: (0, 0))],
        out_specs=row,
        compiler_params=pltpu.CompilerParams(
            dimension_semantics=("arbitrary",), vmem_limit_bytes=VMEM_LIMIT_BYTES),
        name="shared_ffn",
    )(x, wg, wu, wd)


def _ffn_items(counts, n_rows):
    n_tiles = n_rows // MOE_TILE
    n_items = n_tiles + N_EXPERTS - 1
    end = jnp.cumsum(counts)
    start = end - counts
    first_tile = start // MOE_TILE
    tiles_of = jnp.where(counts > 0, (end - 1) // MOE_TILE - first_tile + 1, 0)
    item_end = jnp.cumsum(tiles_of)
    item_start = item_end - tiles_of
    total = item_end[-1]
    w = jnp.minimum(jnp.arange(n_items, dtype=jnp.int32), total - 1)
    exp = jnp.sum((item_end[None, :] <= w[:, None]).astype(jnp.int32), axis=1)
    onehot = (exp[:, None] == jnp.arange(N_EXPERTS, dtype=jnp.int32)[None, :]).astype(jnp.int32)
    pick = lambda v: jnp.sum(onehot * v[None, :], axis=1)
    tile = pick(first_tile) + (w - pick(item_start))
    lo = jnp.maximum(pick(start) - tile * MOE_TILE, 0)
    hi = jnp.minimum(pick(end) - tile * MOE_TILE, MOE_TILE)
    prev_tile = jnp.concatenate([jnp.full((1,), -1, jnp.int32), tile[:-1]])
    first = (tile != prev_tile).astype(jnp.int32)
    prev_exp = jnp.concatenate([jnp.full((1,), -1, jnp.int32), exp[:-1].astype(jnp.int32)])
    newexp = (exp != prev_exp).astype(jnp.int32)
    parity = (jnp.cumsum(newexp) - 1) % 2
    ids = jnp.arange(N_EXPERTS, dtype=jnp.int32)
    later = (ids[None, :] > ids[:, None]) & (counts[None, :] > 0)
    next_of = jnp.min(jnp.where(later, ids[None, :], N_EXPERTS), axis=1)
    next_of = jnp.where(next_of == N_EXPERTS, -1, next_of)
    i32 = lambda v: v.astype(jnp.int32)
    return start.astype(jnp.int32), (i32(tile), i32(exp), i32(lo), i32(hi), first, i32(total).reshape(1),
                                     newexp, i32(pick(next_of)), i32(parity))


def moe_sublayer(x, t, mods, ln_g, ln_b, layout, layer, w_router, router_bias, w_gate, w_up, w_down,
                 ws_gate, ws_up, ws_down):
    tt, dm = t.shape
    eidx, pos, gate, counts = route(t, w_router, router_bias)
    start, items = _ffn_items(counts, tt * TOP_K)
    ids = jnp.arange(N_EXPERTS, dtype=jnp.int32)
    slot = pos + jnp.sum(jnp.where(eidx[..., None] == ids, start, 0), axis=-1)
    xs = dispatch(t, slot)
    ys = expert_ffn(xs, items, w_gate, w_up, w_down, layer)
    shared = shared_ffn(t, ws_gate, ws_up, ws_down)
    gate_cols = jnp.swapaxes(gate, 1, 2).reshape(tt, 8)
    return combine(shared, ys, gate_cols, slot, x, mods, ln_g, ln_b, layout)


TOKEN_TILE = 256
MOD_SHIFT1, MOD_SCALE1, MOD_GATE1, MOD_SHIFT2, MOD_SCALE2, MOD_GATE2 = range(6)


class Layout:
    def __init__(self, n_ctx_seq, ctx_len, n_lat_seq, lat_len):
        self.n_ctx_seq, self.ctx_len, self.n_lat_seq, self.lat_len = n_ctx_seq, ctx_len, n_lat_seq, lat_len
        self.n_ctx = n_ctx_seq * ctx_len
        self.n_tok = self.n_ctx + n_lat_seq * lat_len
        assert ctx_len % TOKEN_TILE == 0 and lat_len % TOKEN_TILE == 0

    def cond_row(self, tile):
        ctx_tiles = self.n_ctx // TOKEN_TILE
        return jnp.where(tile < ctx_tiles, 0, 1 + (tile - ctx_tiles) // (self.lat_len // TOKEN_TILE))

    def tile_pos(self, tile):
        ctx_tiles = self.n_ctx // TOKEN_TILE
        per_ctx, per_lat = self.ctx_len // TOKEN_TILE, self.lat_len // TOKEN_TILE
        is_ctx = tile < ctx_tiles
        return (jnp.where(is_ctx, tile % per_ctx, (tile - ctx_tiles) % per_lat),
                jnp.where(is_ctx, per_ctx, per_lat))


def _premix_kernel(x_ref, prev_ref, next_ref, mod_ref, mp_ref, mn_ref, *out_refs, layout):
    tm = x_ref.shape[0]
    pos, per_seq = layout.tile_pos(pl.program_id(0))
    shift, scale = mod_ref[MOD_SHIFT1:MOD_SHIFT1 + 1, :], mod_ref[MOD_SCALE1:MOD_SCALE1 + 1, :]
    mod = lambda t: t * (1.0 + scale) + shift
    h = mod(x_ref[...])
    h_before = jnp.where(pos > 0, mod(prev_ref[...]), 0.0)
    h_after = jnp.where(pos < per_seq - 1, mod(next_ref[...]), 0.0)
    row = lax.broadcasted_iota(jnp.int32, h.shape, 0)
    d_prev = jnp.where(row == 0, h_before, pltpu.roll(h, 1, 0)) - h
    d_next = jnp.where(row == tm - 1, h_after, pltpu.roll(h, tm - 1, 0)) - h
    for i, o_ref in enumerate(out_refs):
        o_ref[...] = (h + mp_ref[i:i + 1, :] * d_prev + mn_ref[i:i + 1, :] * d_next).astype(BF16)


def premix(x, mods, mix_prev, mix_next, layout):
    tt, dm = x.shape
    nt = tt // TOKEN_TILE
    zero = jnp.zeros((1, dm), F32)
    prev_rows = jnp.concatenate([zero, x[TOKEN_TILE - 1::TOKEN_TILE][:-1]], axis=0)[:, None, :]
    next_rows = jnp.concatenate([x[::TOKEN_TILE][1:], zero], axis=0)[:, None, :]
    tile = pl.BlockSpec((TOKEN_TILE, dm), lambda i: (i, 0))
    edge = pl.BlockSpec((None, 1, dm), lambda i: (i, 0, 0))
    par = pl.BlockSpec((6, dm), lambda i: (0, 0))
    return pl.pallas_call(
        functools.partial(_premix_kernel, layout=layout),
        out_shape=[jax.ShapeDtypeStruct((tt, dm), BF16)] * 6,
        grid=(nt,),
        in_specs=[tile, edge, edge, pl.BlockSpec((None, 6, dm), lambda i: (layout.cond_row(i), 0, 0)), par, par],
        out_specs=[tile] * 6,
        compiler_params=pltpu.CompilerParams(
            dimension_semantics=("arbitrary",), vmem_limit_bytes=VMEM_LIMIT_BYTES),
        name="premix",
    )(x, prev_rows, next_rows, mods, mix_prev, mix_next)


def _residual_ln(x, y, gate, g, b):
    z = ALPHA * x + gate * y
    mu = jnp.mean(z, axis=-1, keepdims=True)
    zc = z - mu
    var = jnp.mean(zc * zc, axis=-1, keepdims=True)
    return zc * lax.rsqrt(var + LN_EPS) * g + b


def _post_kernel(x_ref, y_ref, mod_ref, g_ref, b_ref, *out_refs, gate_row, mod_rows):
    xn = _residual_ln(x_ref[...], y_ref[...], mod_ref[gate_row:gate_row + 1, :], g_ref[...], b_ref[...])
    out_refs[0][...] = xn
    if mod_rows is not None:
        sh, sc = mod_rows
        out_refs[1][...] = xn * (1.0 + mod_ref[sc:sc + 1, :]) + mod_ref[sh:sh + 1, :]


def post(x, y, mods, ln_g, ln_b, layout, gate_row, mod_rows):
    tt, dm = x.shape
    tile = pl.BlockSpec((TOKEN_TILE, dm), lambda i: (i, 0))
    vec = pl.BlockSpec((1, dm), lambda i: (0, 0))
    n_out = 1 if mod_rows is None else 2
    return pl.pallas_call(
        functools.partial(_post_kernel, gate_row=gate_row, mod_rows=mod_rows),
        out_shape=[jax.ShapeDtypeStruct((tt, dm), F32)] * n_out,
        grid=(tt // TOKEN_TILE,),
        in_specs=[tile, tile, pl.BlockSpec((None, 6, dm), lambda i: (layout.cond_row(i), 0, 0)), vec, vec],
        out_specs=[tile] * n_out,
        compiler_params=pltpu.CompilerParams(
            dimension_semantics=("arbitrary",), vmem_limit_bytes=VMEM_LIMIT_BYTES),
        name="post",
    )(x, y, mods, ln_g[None, :], ln_b[None, :])


def _pool_kernel(x_ref, mod_ref, w_ref, scale_ref, o_ref, *, grid_w):
    seq_len = x_ref.shape[0]
    t_idx = lax.broadcasted_iota(jnp.int32, x_ref.shape, 0)

    def shifted(v, k, pos, extent, stride):
        if k > 0:
            return jnp.where(pos < extent - k, pltpu.roll(v, seq_len - k * stride, 0), 0.0)
        return jnp.where(pos >= -k, pltpu.roll(v, -k * stride, 0), 0.0)

    def box(v, w, pos, extent, stride):
        m = w // 2
        ahead, behind, step = v, v, 1
        while step < m:
            ahead = ahead + shifted(ahead, step, pos, extent, stride)
            behind = behind + shifted(behind, -step, pos, extent, stride)
            step *= 2
        total = ahead + shifted(behind, -1, pos, extent, stride)
        cnt = jnp.minimum(pos + m, extent) - jnp.maximum(pos - m, 0)
        return total, cnt.astype(F32)

    h = x_ref[...] * (1.0 + mod_ref[MOD_SCALE1:MOD_SCALE1 + 1, :]) + mod_ref[MOD_SHIFT1:MOD_SHIFT1 + 1, :]
    for gi, w in enumerate(POOL_WINDOWS):
        @pl.when(pl.program_id(1) == gi)
        def _(w=w):
            if grid_w is None:
                s, cnt = box(h, w, t_idx, seq_len, 1)
            else:
                s, cr = box(h, w, t_idx // grid_w, seq_len // grid_w, grid_w)
                s, cc = box(s, w, t_idx % grid_w, grid_w, 1)
                cnt = cr * cc
            p = (s / cnt - h).astype(BF16)
            o_ref[...] = _dot(p, w_ref[...].astype(BF16)) * scale_ref[...]


def pool_mix(x, mods, pool_w, pool_scale, layout, latent):
    dm = x.shape[1]
    n_grp = len(POOL_WINDOWS)
    if latent:
        n_seq, seq_len, first_blk, cond0 = layout.n_lat_seq, layout.lat_len, layout.n_ctx // layout.lat_len, 1
    else:
        n_seq, seq_len, first_blk, cond0 = layout.n_ctx_seq, layout.ctx_len, 0, None
    cond = (lambda i: 0) if cond0 is None else (lambda i: cond0 + i)
    return pl.pallas_call(
        functools.partial(_pool_kernel, grid_w=GRID_W if latent else None),
        out_shape=jax.ShapeDtypeStruct((n_seq * seq_len, dm), F32),
        grid=(n_seq, n_grp),
        in_specs=[pl.BlockSpec((seq_len, POOL_GROUP), lambda i, g: (first_blk + i, g)),
                  pl.BlockSpec((None, 6, POOL_GROUP), lambda i, g: (cond(i), 0, g)),
                  pl.BlockSpec((None, POOL_GROUP, POOL_GROUP), lambda i, g: (g, 0, 0)),
                  pl.BlockSpec((1, POOL_GROUP), lambda i, g: (0, g))],
        out_specs=pl.BlockSpec((seq_len, POOL_GROUP), lambda i, g: (i, g)),
        compiler_params=pltpu.CompilerParams(
            dimension_semantics=("arbitrary", "arbitrary"), vmem_limit_bytes=VMEM_LIMIT_BYTES),
        name="pool_mix",
    )(x, mods, pool_w, pool_scale[None, :])


def kernel(x_prompt, x_sample, c, state_rwkv, c_ctx, w_ada, b_ada, ln_g, ln_b, rw_mix_prev, rw_mix_next, rw_w_r, rw_w_k, rw_w_v, rw_w_o, rw_w0, rw_w1, rw_w2, rw_a0, rw_a1, rw_a2, rw_g1, rw_g2, rw_k_k, rw_k_a, rw_r_k, rw_gn_g, rw_gn_b, pool_w, pool_scale, moe_router, moe_router_bias, moe_w_gate, moe_w_up, moe_w_down, moe_ws_gate, moe_ws_up, moe_ws_down):
    dm = D_MODEL
    H, N = RWKV_HEADS, RWKV_HEAD
    bc, lc_, _ = x_prompt.shape
    bl, ll, _ = x_sample.shape
    layout = Layout(bc, lc_, bl, ll)
    tc = layout.n_ctx
    x = jnp.concatenate([x_prompt.reshape(tc, dm), x_sample.reshape(bl * ll, dm)], axis=0)
    cond = jnp.concatenate([c_ctx[None, :], c], axis=0)
    n_cond = cond.shape[0]
    assert n_cond <= 8
    cond_pad = jnp.zeros((8, dm), F32).at[:n_cond].set(jax.nn.silu(cond))
    new_states = []

    for l in range(DEPTH):
        mods = (matmul(cond_pad, w_ada, tn=1024, layer=l) + b_ada[l]).reshape(8, 6, dm)
        j = l // 2
        if l % 2 == 0:
            xr, xw, xk, xv, xa, xg = premix(x, mods, rw_mix_prev[j], rw_mix_next[j], layout)
            r = matmul(xr, rw_w_r[j])
            k = matmul(xk, rw_w_k[j])
            v = matmul(xv, rw_w_v[j])
            g = lora(xg, rw_g1[j], rw_g2[j], jax.nn.sigmoid)
            wl = [lora(xw, rw_w1[j, d], rw_w2[j, d], jnp.tanh) for d in range(2)]
            al = [lora(xa, rw_a1[j, d], rw_a2[j, d]) for d in range(2)]
            par = jnp.zeros((16, dm), F32).at[:9].set(jnp.stack(
                [rw_w0[j, 0], rw_w0[j, 1], rw_a0[j, 0], rw_a0[j, 1], rw_k_k[j], rw_k_a[j],
                 rw_r_k[j].reshape(dm), rw_gn_g[j], rw_gn_b[j]]))
            out_c, s_fin = rwkv_core(r, k, v, g, wl, al, par, jnp.zeros((bc, 2, H, N, N), F32), 0, lc_)
            out_l, _ = rwkv_core(r, k, v, g, wl, al, par, state_rwkv[:, j], tc, ll)
            new_states.append(s_fin)
            mix = matmul(jnp.concatenate([out_c, out_l], axis=0), rw_w_o[j])
        else:
            mix = jnp.concatenate([pool_mix(x, mods, pool_w[j], pool_scale[j], layout, False),
                                   pool_mix(x, mods, pool_w[j], pool_scale[j], layout, True)], axis=0)
        x, h2 = post(x, mix, mods, ln_g[l, 0], ln_b[l, 0], layout, MOD_GATE1, (MOD_SHIFT2, MOD_SCALE2))
        x = moe_sublayer(x, h2, mods, ln_g[l, 1], ln_b[l, 1], layout, l, moe_router[l], moe_router_bias[l],
                         moe_w_gate, moe_w_up, moe_w_down, moe_ws_gate[l], moe_ws_up[l], moe_ws_down[l])

    return (x[:tc].reshape(bc, lc_, dm), x[tc:].reshape(bl, ll, dm), jnp.stack(new_states, axis=1))
```
